```python
import math
import jax, jax.numpy as jnp
from jax import lax
import numpy as np

D_MODEL = 1024
BATCH = 8
SEQ = 4096
DEPTH = 4

N_EVEN = (DEPTH + 1) // 2
N_ODD = DEPTH // 2
EPS = 1e-6

A_WIDTH = D_MODEL // 2
A_HEADS = 4
A_HEAD_DIM = A_WIDTH // A_HEADS
CHUNK = 128
B_WIDTH = D_MODEL - A_WIDTH
B_GROUP = 16
B_GROUPS = B_WIDTH // B_GROUP
B_STATE = 64
DT_MIN = 1e-3
DT_MAX = 1e-1
LAMBDA_RE_MAX = -1e-4
EVEN_IN = 2 * A_WIDTH + B_WIDTH
C_HEADS = 8
C_HEAD_DIM = D_MODEL // C_HEADS // 2
C_V_DIM = 2 * C_HEAD_DIM
Q_BLOCK = 128
ODD_IN = 3 * D_MODEL
D_FF = 2816
N_EXPERTS = 8
TOP_K = 2
D_FF_EXPERT = 3584

kernel_name = 'hybrid_gmlp_s5_diffattn_moe'


def rms_norm(x, g):
    xf = x.astype(jnp.float32)
    y = xf * lax.rsqrt(jnp.mean(xf * xf, axis=-1, keepdims=True) + EPS)
    return (y * g.astype(jnp.float32)).astype(x.dtype)


def layer_norm(x, g, b):
    xf = x.astype(jnp.float32)
    mu = jnp.mean(xf, axis=-1, keepdims=True)
    xc = xf - mu
    y = xc * lax.rsqrt(jnp.mean(xc * xc, axis=-1, keepdims=True) + EPS)
    return (y * g.astype(jnp.float32) + b.astype(jnp.float32)).astype(x.dtype)


def swiglu(h, w_gate, w_up, w_down):
    return (jax.nn.silu(h @ w_gate) * (h @ w_up)) @ w_down


def chunked_gmlp(z, ln_g, ln_b, w_s, b_s):
    bsz, seq, _ = z.shape
    u, v = jnp.split(z, 2, axis=-1)
    v = layer_norm(v, ln_g, ln_b)
    v = v.reshape(bsz, seq // CHUNK, CHUNK, A_HEADS, A_HEAD_DIM)
    mask = jnp.tril(jnp.ones((CHUNK, CHUNK), dtype=w_s.dtype))
    s = jnp.einsum('hts,bcshe->bcthe', w_s * mask, v) + b_s.T[None, None, :, :, None]
    return u * s.reshape(bsz, seq, A_WIDTH)


def _ssm_combine(left, right):
    a_l, b_l = left
    a_r, b_r = right
    return a_r * a_l, a_r * b_l + b_r


def s5_mixer(u, lam_re, lam_im, log_dt, b_re, b_im, c_re, c_im, d, w_glu, b_glu):
    f32 = jnp.float32
    bsz, seq, _ = u.shape
    uf = u.astype(f32).reshape(bsz, seq, B_GROUPS, B_GROUP)
    lam = lax.complex(jnp.minimum(lam_re.astype(f32), LAMBDA_RE_MAX), lam_im.astype(f32))
    dt = jnp.exp(log_dt.astype(f32))[:, None]
    lam_bar = jnp.exp(lam * dt)
    b_bar = ((lam_bar - 1.0) / lam)[..., None] * lax.complex(b_re.astype(f32), b_im.astype(f32))
    c_mat = lax.complex(c_re.astype(f32), c_im.astype(f32))
    bu = jnp.einsum('gpc,blgc->blgp', b_bar, uf.astype(jnp.complex64))
    a = jnp.broadcast_to(lam_bar[None, None], (1, seq, B_GROUPS, B_STATE))
    _, states = lax.associative_scan(_ssm_combine, (a, bu), axis=1)
    y = jnp.real(jnp.einsum('gcp,blgp->blgc', c_mat, states)) + d.astype(f32).reshape(B_GROUPS, B_GROUP) * uf
    y = jax.nn.gelu(y.reshape(bsz, seq, B_WIDTH))
    y = y * jax.nn.sigmoid(y @ w_glu.astype(f32) + b_glu.astype(f32))
    return y.astype(u.dtype)


def diff_attention(h, w_in, q_g, k_g, lq1, lk1, lq2, lk2, subln_g, lambda_init):
    f32 = jnp.float32
    bsz, seq, _ = h.shape
    q, k, v = jnp.split(h @ w_in, 3, axis=-1)
    q = q.reshape(bsz, seq, C_HEADS, 2, C_HEAD_DIM)
    k = k.reshape(bsz, seq, C_HEADS, 2, C_HEAD_DIM)
    q = rms_norm(q, q_g).astype(f32) * (C_HEAD_DIM ** -0.5)
    k = rms_norm(k, k_g).astype(f32)
    q = q.transpose(0, 2, 3, 1, 4)
    k = k.transpose(0, 2, 3, 1, 4)
    v = v.reshape(bsz, seq, C_HEADS, C_V_DIM).astype(f32).transpose(0, 2, 1, 3)
    lam = (jnp.exp(jnp.sum(lq1.astype(f32) * lk1.astype(f32)))
           - jnp.exp(jnp.sum(lq2.astype(f32) * lk2.astype(f32))) + lambda_init)
    outs = []
    for blk in range(seq // Q_BLOCK):
        start = blk * Q_BLOCK
        end = start + Q_BLOCK
        s = jnp.einsum('bhmqd,bhmkd->bhmqk', q[:, :, :, start:end], k[:, :, :, :end])
        mask = jnp.arange(end)[None, :] <= (start + jnp.arange(Q_BLOCK))[:, None]
        p = jax.nn.softmax(jnp.where(mask, s, -jnp.inf), axis=-1)
        attn = p[:, :, 0] - lam * p[:, :, 1]
        outs.append(jnp.einsum('bhqk,bhkd->bhqd', attn, v[:, :, :end]))
    o = jnp.concatenate(outs, axis=2)
    o = rms_norm(o, subln_g) * (1.0 - lambda_init)
    return o.transpose(0, 2, 1, 3).reshape(bsz, seq, C_HEADS * C_V_DIM).astype(h.dtype)


def moe_swiglu(h, router, w_gate, w_up, w_down):
    bsz, seq, d = h.shape
    t = h.reshape(-1, d)
    logits = (t @ router).astype(jnp.float32)
    top_vals, top_idx = lax.top_k(logits, TOP_K)
    top_w = jax.nn.softmax(top_vals, axis=-1)
    gates = jnp.sum(jax.nn.one_hot(top_idx, N_EXPERTS, dtype=jnp.float32) * top_w[..., None], axis=1)
    gates = gates.astype(t.dtype)
    out = jnp.zeros_like(t)
    for e in range(N_EXPERTS):
        out = out + gates[:, e:e + 1] * swiglu(t, w_gate[e], w_up[e], w_down[e])
    return out.reshape(bsz, seq, d)


def setup_inputs(seed: int = 0) -> dict:
    key = jax.random.key(seed)
    ks = iter(jax.random.split(key, 48))
    f32 = jnp.float32

    def nrm(shape, scale):
        return scale * jax.random.normal(next(ks), shape, f32)

    def gain(shape):
        return 1.0 + nrm(shape, 0.01)

    E, O, D = N_EVEN, N_ODD, D_MODEL
    inp = {}
    inp['x'] = nrm((BATCH, SEQ, D), 1.0)
    inp['ev_norm_mix'] = gain((E, D))
    inp['ev_w_in'] = nrm((E, D, EVEN_IN), D ** -0.5)
    inp['ev_gmlp_ln_g'] = gain((E, A_WIDTH))
    inp['ev_gmlp_ln_b'] = nrm((E, A_WIDTH), 0.01)
    inp['ev_gmlp_ws'] = nrm((E, A_HEADS, CHUNK, CHUNK), CHUNK ** -0.5)
    inp['ev_gmlp_bs'] = gain((E, A_HEADS, CHUNK))
    inp['ev_s5_lambda_re'] = -0.5 + nrm((E, B_GROUPS, B_STATE), 0.01)
    inp['ev_s5_lambda_im'] = jnp.pi * jnp.arange(B_STATE, dtype=f32) + nrm((E, B_GROUPS, B_STATE), 0.01)
    inp['ev_s5_log_dt'] = jax.random.uniform(next(ks), (E, B_GROUPS), f32, math.log(DT_MIN), math.log(DT_MAX))
    inp['ev_s5_b_re'] = nrm((E, B_GROUPS, B_STATE, B_GROUP), (2 * B_GROUP) ** -0.5)
    inp['ev_s5_b_im'] = nrm((E, B_GROUPS, B_STATE, B_GROUP), (2 * B_GROUP) ** -0.5)
    inp['ev_s5_c_re'] = nrm((E, B_GROUPS, B_GROUP, B_STATE), (2 * B_STATE) ** -0.5)
    inp['ev_s5_c_im'] = nrm((E, B_GROUPS, B_GROUP, B_STATE), (2 * B_STATE) ** -0.5)
    inp['ev_s5_d'] = nrm((E, B_WIDTH), 1.0)
    inp['ev_s5_w_glu'] = nrm((E, B_WIDTH, B_WIDTH), B_WIDTH ** -0.5)
    inp['ev_s5_b_glu'] = nrm((E, B_WIDTH), 0.01)
    inp['ev_w_out'] = nrm((E, A_WIDTH + B_WIDTH, D), (A_WIDTH + B_WIDTH) ** -0.5)
    inp['ev_norm_ffn'] = gain((E, D))
    inp['ev_ffn_w_gate'] = nrm((E, D, D_FF), D ** -0.5)
    inp['ev_ffn_w_up'] = nrm((E, D, D_FF), D ** -0.5)
    inp['ev_ffn_w_down'] = nrm((E, D_FF, D), D_FF ** -0.5)
    inp['od_norm_mix'] = gain((O, D))
    inp['od_w_in'] = nrm((O, D, ODD_IN), D ** -0.5)
    inp['od_q_norm'] = gain((O, C_HEAD_DIM))
    inp['od_k_norm'] = gain((O, C_HEAD_DIM))
    inp['od_lambda_q1'] = nrm((O, C_HEAD_DIM), 0.1)
    inp['od_lambda_k1'] = nrm((O, C_HEAD_DIM), 0.1)
    inp['od_lambda_q2'] = nrm((O, C_HEAD_DIM), 0.1)
    inp['od_lambda_k2'] = nrm((O, C_HEAD_DIM), 0.1)
    inp['od_subln'] = gain((O, C_V_DIM))
    inp['od_w_out'] = nrm((O, C_HEADS * C_V_DIM, D), (C_HEADS * C_V_DIM) ** -0.5)
    inp['od_norm_ffn'] = gain((O, D))
    inp['od_router'] = nrm((O, D, N_EXPERTS), D ** -0.5)
    inp['od_moe_w_gate'] = nrm((O, N_EXPERTS, D, D_FF_EXPERT), D ** -0.5)
    inp['od_moe_w_up'] = nrm((O, N_EXPERTS, D, D_FF_EXPERT), D ** -0.5)
    inp['od_moe_w_down'] = nrm((O, N_EXPERTS, D_FF_EXPERT, D), D_FF_EXPERT ** -0.5)
    return inp


def reference(x, ev_norm_mix, ev_w_in, ev_gmlp_ln_g, ev_gmlp_ln_b, ev_gmlp_ws, ev_gmlp_bs,
              ev_s5_lambda_re, ev_s5_lambda_im, ev_s5_log_dt, ev_s5_b_re, ev_s5_b_im,
              ev_s5_c_re, ev_s5_c_im, ev_s5_d, ev_s5_w_glu, ev_s5_b_glu, ev_w_out,
              ev_norm_ffn, ev_ffn_w_gate, ev_ffn_w_up, ev_ffn_w_down,
              od_norm_mix, od_w_in, od_q_norm, od_k_norm, od_lambda_q1, od_lambda_k1,
              od_lambda_q2, od_lambda_k2, od_subln, od_w_out, od_norm_ffn, od_router,
              od_moe_w_gate, od_moe_w_up, od_moe_w_down):
    for layer in range(DEPTH):
        i = layer // 2
        if layer % 2 == 0:
            h = rms_norm(x, ev_norm_mix[i])
            z = h @ ev_w_in[i]
            y_a = chunked_gmlp(jax.nn.gelu(z[..., :2 * A_WIDTH]), ev_gmlp_ln_g[i], ev_gmlp_ln_b[i],
                               ev_gmlp_ws[i], ev_gmlp_bs[i])
            y_b = s5_mixer(z[..., 2 * A_WIDTH:], ev_s5_lambda_re[i], ev_s5_lambda_im[i], ev_s5_log_dt[i],
                           ev_s5_b_re[i], ev_s5_b_im[i], ev_s5_c_re[i], ev_s5_c_im[i], ev_s5_d[i],
                           ev_s5_w_glu[i], ev_s5_b_glu[i])
            x = x + jnp.concatenate([y_a, y_b], axis=-1) @ ev_w_out[i]
            x = x + swiglu(rms_norm(x, ev_norm_ffn[i]), ev_ffn_w_gate[i], ev_ffn_w_up[i], ev_ffn_w_down[i])
        else:
            lambda_init = 0.8 - 0.6 * math.exp(-0.3 * layer)
            h = rms_norm(x, od_norm_mix[i])
            y_c = diff_attention(h, od_w_in[i], od_q_norm[i], od_k_norm[i], od_lambda_q1[i], od_lambda_k1[i],
                                 od_lambda_q2[i], od_lambda_k2[i], od_subln[i], lambda_init)
            x = x + y_c @ od_w_out[i]
            x = x + moe_swiglu(rms_norm(x, od_norm_ffn[i]), od_router[i], od_moe_w_gate[i],
                               od_moe_w_up[i], od_moe_w_down[i])
    return x
```

```python
import functools
import math

import jax
import jax.numpy as jnp
from jax import lax
from jax.experimental import pallas as pl
from jax.experimental.pallas import tpu as pltpu

F32 = jnp.float32
BF16 = jnp.bfloat16

EPS = 1e-6
A_HEADS = 4
GMLP_CHUNK = 128
S5_GROUP = 16
S5_STATE = 64
S5_BLOCK = 16
DT_LAMBDA_RE_MAX = -1e-4
ATT_HEADS = 8
ATT_HEAD_DIM = 64
N_EXPERTS = 8
ROUTER_LANES = 128
VMEM_LIMIT = 56 * 1024 * 1024


def _params(*sem):
    return pltpu.CompilerParams(dimension_semantics=sem, vmem_limit_bytes=VMEM_LIMIT)


def _rms(x, g):
    return x * lax.rsqrt(jnp.mean(x * x, axis=-1, keepdims=True) + EPS) * g


def _gelu(x):
    return 0.5 * x * (1.0 + jnp.tanh(0.7978845608028654 * (x + 0.044715 * (x * x * x))))


def _sigmoid(x):
    return 1.0 / (1.0 + jnp.exp(-x))


def _dot(a, b):
    return jnp.dot(a, b, preferred_element_type=F32)


def _dot_nt(a, b):
    return lax.dot_general(a, b, (((1,), (1,)), ((), ())), preferred_element_type=F32)


def _ev_in_kernel(x_ref, gn_ref, w_ref, lng_ref, lnb_ref, wm_ref, bias_ref, ya_ref, zb_ref, *, n_chunk, a_width):
    h = _rms(x_ref[...], gn_ref[...]).astype(BF16)
    z = _dot(h, w_ref[...])
    zb_ref[...] = z[:, 2 * a_width:].astype(BF16)
    g = _gelu(z[:, :2 * a_width])
    u = g[:, :a_width]
    v = g[:, a_width:]
    mu = jnp.mean(v, axis=-1, keepdims=True)
    vc = v - mu
    vn = vc * lax.rsqrt(jnp.mean(vc * vc, axis=-1, keepdims=True) + EPS) * lng_ref[...] + lnb_ref[...]
    vb = vn.astype(BF16)
    hd = a_width // A_HEADS
    for c in range(n_chunk):
        rows = slice(c * GMLP_CHUNK, (c + 1) * GMLP_CHUNK)
        for k in range(A_HEADS):
            cols = slice(k * hd, (k + 1) * hd)
            s = _dot(wm_ref[k], vb[rows, cols]) + bias_ref[:, cols]
            ya_ref[rows, cols] = (u[rows, cols] * s).astype(BF16)


def _ev_in(x2, gn, w_in, ln_g, ln_b, wm, bias, *, tm):
    n, d = x2.shape
    cols = w_in.shape[1]
    a_width = ln_g.shape[-1]
    b_width = cols - 2 * a_width
    kern = functools.partial(_ev_in_kernel, n_chunk=tm // GMLP_CHUNK, a_width=a_width)
    return pl.pallas_call(
        kern,
        grid=(n // tm,),
        in_specs=[
            pl.BlockSpec((tm, d), lambda i: (i, 0)),
            pl.BlockSpec((1, d), lambda i: (0, 0)),
            pl.BlockSpec((d, cols), lambda i: (0, 0)),
            pl.BlockSpec((1, a_width), lambda i: (0, 0)),
            pl.BlockSpec((1, a_width), lambda i: (0, 0)),
            pl.BlockSpec((A_HEADS, GMLP_CHUNK, GMLP_CHUNK), lambda i: (0, 0, 0)),
            pl.BlockSpec((GMLP_CHUNK, a_width), lambda i: (0, 0)),
        ],
        out_specs=[
            pl.BlockSpec((tm, a_width), lambda i: (i, 0)),
            pl.BlockSpec((tm, b_width), lambda i: (i, 0)),
        ],
        out_shape=[
            jax.ShapeDtypeStruct((n, a_width), BF16),
            jax.ShapeDtypeStruct((n, b_width), BF16),
        ],
        compiler_params=_params("parallel"),
        name="ev_in_gmlp",
    )(x2, gn, w_in, ln_g, ln_b, wm, bias)


def _s5_kernel(u_ref, tt_ref, pt_ref, qt_ref, ar_ref, ai_ref, o_ref, *, nb, nsteps):
    r = u_ref.shape[-1]
    rows = S5_BLOCK * S5_GROUP
    lane = lax.broadcasted_iota(jnp.int32, (S5_STATE, r), 1)

    def shifted(a, sh):
        return jnp.where(lane >= sh, pltpu.roll(a, sh, 1), 0.0)

    for b in range(nb):
        ut = u_ref[b].reshape(rows, r)
        st = _dot(pt_ref[...], ut)
        xr = st[:S5_STATE]
        xi = st[S5_STATE:]
        for j in range(nsteps):
            sh = 1 << j
            ar = ar_ref[j]
            ai = ai_ref[j]
            sr = shifted(xr, sh)
            si = shifted(xi, sh)
            xr, xi = xr + ar * sr - ai * si, xi + ar * si + ai * sr
        xprev = jnp.concatenate([shifted(xr, 1), shifted(xi, 1)], axis=0).astype(BF16)
        y = _dot(tt_ref[...], ut) + _dot(qt_ref[...], xprev)
        o_ref[b] = _gelu(y).astype(BF16).reshape(S5_BLOCK, S5_GROUP, r)


def _s5(zbt, tt, pt, qt, ar, ai):
    bsz, _, width, r = zbt.shape
    groups = width // S5_GROUP
    nsteps = ar.shape[1]
    u5 = zbt.reshape(bsz, S5_BLOCK, groups, S5_GROUP, r)
    rows = S5_BLOCK * S5_GROUP
    kern = functools.partial(_s5_kernel, nb=bsz, nsteps=nsteps)
    out = pl.pallas_call(
        kern,
        grid=(groups,),
        in_specs=[
            pl.BlockSpec((bsz, S5_BLOCK, None, S5_GROUP, r), lambda g: (0, 0, g, 0, 0)),
            pl.BlockSpec((None, rows, rows), lambda g: (g, 0, 0)),
            pl.BlockSpec((None, 2 * S5_STATE, rows), lambda g: (g, 0, 0)),
            pl.BlockSpec((None, rows, 2 * S5_STATE), lambda g: (g, 0, 0)),
            pl.BlockSpec((None, nsteps, S5_STATE, 1), lambda g: (g, 0, 0, 0)),
            pl.BlockSpec((None, nsteps, S5_STATE, 1), lambda g: (g, 0, 0, 0)),
        ],
        out_specs=pl.BlockSpec((bsz, S5_BLOCK, None, S5_GROUP, r), lambda g: (0, 0, g, 0, 0)),
        out_shape=jax.ShapeDtypeStruct(u5.shape, BF16),
        compiler_params=_params("parallel"),
        name="ev_s5",
    )(u5, tt, pt, qt, ar, ai)
    return out.reshape(zbt.shape)


def _s5_operators(lam_re, lam_im, log_dt, b_re, b_im, c_re, c_im, d, nsteps):
    groups, p = lam_re.shape
    c = S5_GROUP
    ln = S5_BLOCK
    lr = jnp.minimum(lam_re.astype(F32), DT_LAMBDA_RE_MAX)
    li = lam_im.astype(F32)
    dt = jnp.exp(log_dt.astype(F32))[:, None]
    zr = lr * dt
    zi = li * dt

    def powers(k):
        kk = jnp.asarray(k, F32)[None, :, None]
        mag = jnp.exp(zr[:, None, :] * kk)
        ang = zi[:, None, :] * kk
        return mag * jnp.cos(ang), mag * jnp.sin(ang)

    lbr, lbi = powers(jnp.ones((1,)))
    lbr = lbr[:, 0]
    lbi = lbi[:, 0]
    den = lr * lr + li * li
    fr = ((lbr - 1.0) * lr + lbi * li) / den
    fi = (lbi * lr - (lbr - 1.0) * li) / den
    bbr = fr[..., None] * b_re.astype(F32) - fi[..., None] * b_im.astype(F32)
    bbi = fr[..., None] * b_im.astype(F32) + fi[..., None] * b_re.astype(F32)
    cr = c_re.astype(F32)
    ci = c_im.astype(F32)

    pwr, pwi = powers(jnp.arange(ln + 1))
    mr = pwr[:, :ln, :, None] * bbr[:, None] - pwi[:, :ln, :, None] * bbi[:, None]
    mi = pwr[:, :ln, :, None] * bbi[:, None] + pwi[:, :ln, :, None] * bbr[:, None]
    kmat = jnp.einsum("gcp,gkpd->gkcd", cr, mr) - jnp.einsum("gcp,gkpd->gkcd", ci, mi)
    s_idx = jnp.arange(ln)[:, None]
    t_idx = jnp.arange(ln)[None, :]
    lag = t_idx - s_idx
    blocks = kmat[:, jnp.clip(lag, 0, ln - 1)]
    blocks = jnp.where((lag >= 0)[None, :, :, None, None], blocks, 0.0)
    skip = (jnp.eye(ln, dtype=F32)[None, :, :, None, None]
            * jnp.eye(c, dtype=F32)[None, None, None]
            * d.astype(F32).reshape(groups, 1, 1, c, 1))
    blocks = blocks + skip
    tt = blocks.transpose(0, 2, 3, 1, 4).reshape(groups, ln * c, ln * c)

    rev = ln - 1 - jnp.arange(ln)
    pr = pwr[:, rev][:, :, :, None] * bbr[:, None] - pwi[:, rev][:, :, :, None] * bbi[:, None]
    pi = pwr[:, rev][:, :, :, None] * bbi[:, None] + pwi[:, rev][:, :, :, None] * bbr[:, None]
    pt = jnp.concatenate([pr, pi], axis=2).transpose(0, 2, 1, 3).reshape(groups, 2 * p, ln * c)

    a_r = cr[:, None] * pwr[:, 1:ln + 1, None, :] - ci[:, None] * pwi[:, 1:ln + 1, None, :]
    a_i = cr[:, None] * pwi[:, 1:ln + 1, None, :] + ci[:, None] * pwr[:, 1:ln + 1, None, :]
    qt = jnp.concatenate([a_r, -a_i], axis=-1).reshape(groups, ln * c, 2 * p)

    sr, si = powers(ln * (2 ** jnp.arange(nsteps)))
    return (tt.astype(BF16), pt.astype(BF16), qt.astype(BF16), sr[..., None], si[..., None])


def _ev_out_kernel(x_ref, ya_ref, yb_ref, wglu_ref, bglu_ref, woa_ref, wob_ref, o_ref):
    yb = yb_ref[...]
    gate = _sigmoid(_dot(yb, wglu_ref[...]) + bglu_ref[...])
    y2 = (yb.astype(F32) * gate).astype(BF16)
    o_ref[...] = x_ref[...] + _dot(ya_ref[...], woa_ref[...]) + _dot(y2, wob_ref[...])


def _ev_out(x2, ya, yb, wglu, bglu, woa, wob, *, tm):
    n, d = x2.shape
    aw = ya.shape[1]
    bw = yb.shape[1]
    return pl.pallas_call(
        _ev_out_kernel,
        grid=(n // tm,),
        in_specs=[
            pl.BlockSpec((tm, d), lambda i: (i, 0)),
            pl.BlockSpec((tm, aw), lambda i: (i, 0)),
            pl.BlockSpec((tm, bw), lambda i: (i, 0)),
            pl.BlockSpec((bw, bw), lambda i: (0, 0)),
            pl.BlockSpec((1, bw), lambda i: (0, 0)),
            pl.BlockSpec((aw, d), lambda i: (0, 0)),
            pl.BlockSpec((bw, d), lambda i: (0, 0)),
        ],
        out_specs=pl.BlockSpec((tm, d), lambda i: (i, 0)),
        out_shape=jax.ShapeDtypeStruct((n, d), F32),
        compiler_params=_params("parallel"),
        name="ev_out",
    )(x2, ya, yb, wglu, bglu, woa, wob)


def _router_gates(hn, router):
    logits = jnp.dot(hn, router, preferred_element_type=F32, precision=lax.Precision.HIGHEST)
    lane = lax.broadcasted_iota(jnp.int32, logits.shape, 1)
    neg = jnp.float32(-jnp.inf)
    logits = jnp.where(lane < N_EXPERTS, logits, neg)
    m1 = jnp.max(logits, axis=-1, keepdims=True)
    i1 = jnp.min(jnp.where(logits == m1, lane, ROUTER_LANES), axis=-1, keepdims=True)
    rest = jnp.where(lane == i1, neg, logits)
    m2 = jnp.max(rest, axis=-1, keepdims=True)
    i2 = jnp.min(jnp.where(rest == m2, lane, ROUTER_LANES), axis=-1, keepdims=True)
    e2 = jnp.exp(m2 - m1)
    w1 = 1.0 / (1.0 + e2)
    w2 = e2 / (1.0 + e2)
    return jnp.where(lane == i1, w1, 0.0) + jnp.where(lane == i2, w2, 0.0)


def _ffn_kernel(*refs, moe):
    if moe:
        x_ref, gn_ref, router_ref, wg_ref, wu_ref, wd_ref, o_ref, hn_ref, acc_ref, gates_ref = refs
    else:
        x_ref, gn_ref, wg_ref, wu_ref, wd_ref, o_ref, hn_ref, acc_ref = refs
    e = pl.program_id(1)
    f = pl.program_id(2)

    @pl.when((e == 0) & (f == 0))
    def _():
        hn = _rms(x_ref[...], gn_ref[...])
        hn_ref[...] = hn.astype(BF16)
        acc_ref[...] = jnp.zeros_like(acc_ref)
        if moe:
            gates_ref[...] = _router_gates(hn, router_ref[...])

    h = hn_ref[...]
    a = _dot(h, wg_ref[...])
    b = _dot(h, wu_ref[...])
    act = a * _sigmoid(a) * b
    if moe:
        gates = gates_ref[...]
        lane = lax.broadcasted_iota(jnp.int32, gates.shape, 1)
        act = act * jnp.sum(jnp.where(lane == e, gates, 0.0), axis=-1, keepdims=True)
    acc_ref[...] += _dot(act.astype(BF16), wd_ref[...])

    @pl.when((e == pl.num_programs(1) - 1) & (f == pl.num_programs(2) - 1))
    def _():
        o_ref[...] = x_ref[...] + acc_ref[...]


def _ffn(x2, gn, wg, wu, wd, router=None, *, tm, tf):
    n, d = x2.shape
    ne, _, dff = wg.shape
    moe = router is not None
    in_specs = [
        pl.BlockSpec((tm, d), lambda i, e, f: (i, 0)),
        pl.BlockSpec((1, d), lambda i, e, f: (0, 0)),
    ]
    args = [x2, gn]
    scratch = [pltpu.VMEM((tm, d), BF16), pltpu.VMEM((tm, d), F32)]
    if moe:
        in_specs.append(pl.BlockSpec((d, ROUTER_LANES), lambda i, e, f: (0, 0)))
        args.append(router)
        scratch.append(pltpu.VMEM((tm, ROUTER_LANES), F32))
    in_specs += [
        pl.BlockSpec((None, d, tf), lambda i, e, f: (e, 0, f)),
        pl.BlockSpec((None, d, tf), lambda i, e, f: (e, 0, f)),
        pl.BlockSpec((None, tf, d), lambda i, e, f: (e, f, 0)),
    ]
    args += [wg, wu, wd]
    return pl.pallas_call(
        functools.partial(_ffn_kernel, moe=moe),
        grid=(n // tm, ne, dff // tf),
        in_specs=in_specs,
        out_specs=pl.BlockSpec((tm, d), lambda i, e, f: (i, 0)),
        out_shape=jax.ShapeDtypeStruct((n, d), F32),
        scratch_shapes=scratch,
        compiler_params=_params("parallel", "arbitrary", "arbitrary"),
        name="moe_ffn" if moe else "dense_ffn",
    )(*args)


def _qkv_kernel(x_ref, gn_ref, w_ref, seg_ref, gqk_ref, o_ref, *, qk_cols):
    h = _rms(x_ref[...], gn_ref[...]).astype(BF16)
    z = _dot(h, w_ref[...])
    o_ref[:, qk_cols:] = z[:, qk_cols:].astype(BF16)
    seg = seg_ref[...]
    sw = seg.shape[0]
    for c in range(qk_cols // sw):
        cols = slice(c * sw, (c + 1) * sw)
        zc = z[:, cols]
        ms = _dot((zc * zc).astype(BF16), seg) * (1.0 / ATT_HEAD_DIM)
        o_ref[:, cols] = (zc * lax.rsqrt(ms + EPS) * gqk_ref[:, cols]).astype(BF16)


def _qkv(x2, gn, w, seg, gqk, *, tm):
    n, d = x2.shape
    cols = w.shape[1]
    qk_cols = gqk.shape[1]
    return pl.pallas_call(
        functools.partial(_qkv_kernel, qk_cols=qk_cols),
        grid=(n // tm,),
        in_specs=[
            pl.BlockSpec((tm, d), lambda i: (i, 0)),
            pl.BlockSpec((1, d), lambda i: (0, 0)),
            pl.BlockSpec((d, cols), lambda i: (0, 0)),
            pl.BlockSpec(seg.shape, lambda i: (0, 0)),
            pl.BlockSpec((1, qk_cols), lambda i: (0, 0)),
        ],
        out_specs=pl.BlockSpec((tm, cols), lambda i: (i, 0)),
        out_shape=jax.ShapeDtypeStruct((n, cols), BF16),
        compiler_params=_params("parallel"),
        name="od_qkv",
    )(x2, gn, w, seg, gqk)


def _attn_kernel(lam_ref, q_ref, k_ref, v_ref, sg_ref, o_ref, qm_ref, m_ref, l_ref, acc_ref, *, bq, bk, out_scale):
    qi = pl.program_id(2)
    ki = pl.program_id(3)
    last_k = ((qi + 1) * bq - 1) // bk

    @pl.when(ki == 0)
    def _():
        q = q_ref[...]
        lane = lax.broadcasted_iota(jnp.int32, q.shape, 1)
        zero = jnp.zeros_like(q)
        qm_ref[0] = jnp.where(lane < ATT_HEAD_DIM, q, zero)
        qm_ref[1] = jnp.where(lane >= ATT_HEAD_DIM, q, zero)
        m_ref[...] = jnp.full_like(m_ref, -jnp.inf)
        l_ref[...] = jnp.zeros_like(l_ref)
        acc_ref[...] = jnp.zeros_like(acc_ref)

    def step(masked):
        k = k_ref[...]
        v = v_ref[...]
        if masked:
            row = qi * bq + lax.broadcasted_iota(jnp.int32, (bq, bk), 0)
            col = ki * bk + lax.broadcasted_iota(jnp.int32, (bq, bk), 1)
            keep = col <= row
        for m in range(2):
            s = _dot_nt(qm_ref[m], k)
            if masked:
                s = jnp.where(keep, s, -jnp.inf)
            m_old = m_ref[m]
            m_new = jnp.maximum(m_old, jnp.max(s, axis=-1, keepdims=True))
            p = jnp.exp(s - m_new)
            alpha = jnp.exp(m_old - m_new)
            l_ref[m] = alpha * l_ref[m] + jnp.sum(p, axis=-1, keepdims=True)
            acc_ref[m] = alpha * acc_ref[m] + _dot(p.astype(BF16), v)
            m_ref[m] = m_new

    fully_visible = (ki + 1) * bk - 1 <= qi * bq

    @pl.when(fully_visible)
    def _():
        step(False)

    @pl.when(jnp.logical_not(fully_visible) & (ki <= last_k))
    def _():
        step(True)

    @pl.when(ki == last_k)
    def _():
        o = acc_ref[0] / l_ref[0] - lam_ref[0] * (acc_ref[1] / l_ref[1])
        o = _rms(o, sg_ref[...]) * out_scale
        o_ref[...] = o.astype(BF16)


def _attention(qkv3, lam, subln, *, bq, bk, lambda_init):
    bsz, seq, cols = qkv3.shape
    dv = 2 * ATT_HEAD_DIM
    nh = ATT_HEADS

    def kv_index(off):
        def index(b, h, qi, ki):
            return (b, jnp.minimum(ki, ((qi + 1) * bq - 1) // bk), off + h)
        return index

    kern = functools.partial(_attn_kernel, bq=bq, bk=bk, out_scale=1.0 - lambda_init)
    return pl.pallas_call(
        kern,
        grid=(bsz, nh, seq // bq, seq // bk),
        in_specs=[
            pl.BlockSpec(memory_space=pltpu.SMEM),
            pl.BlockSpec((None, bq, dv), lambda b, h, qi, ki: (b, qi, h)),
            pl.BlockSpec((None, bk, dv), kv_index(nh)),
            pl.BlockSpec((None, bk, dv), kv_index(2 * nh)),
            pl.BlockSpec((1, dv), lambda b, h, qi, ki: (0, 0)),
        ],
        out_specs=pl.BlockSpec((None, bq, dv), lambda b, h, qi, ki: (b, qi, h)),
        out_shape=jax.ShapeDtypeStruct((bsz, seq, nh * dv), BF16),
        scratch_shapes=[
            pltpu.VMEM((2, bq, dv), BF16),
            pltpu.VMEM((2, bq, 1), F32),
            pltpu.VMEM((2, bq, 1), F32),
            pltpu.VMEM((2, bq, dv), F32),
        ],
        compiler_params=_params("parallel", "parallel", "parallel", "arbitrary"),
        name="od_attn",
    )(lam, qkv3, qkv3, qkv3, subln)


def _proj_res_kernel(x_ref, y_ref, w_ref, o_ref):
    o_ref[...] = x_ref[...] + _dot(y_ref[...], w_ref[...])


def _proj_res(x2, y, w, *, tm):
    n, d = x2.shape
    k = y.shape[1]
    return pl.pallas_call(
        _proj_res_kernel,
        grid=(n // tm,),
        in_specs=[
            pl.BlockSpec((tm, d), lambda i: (i, 0)),
            pl.BlockSpec((tm, k), lambda i: (i, 0)),
            pl.BlockSpec((k, d), lambda i: (0, 0)),
        ],
        out_specs=pl.BlockSpec((tm, d), lambda i: (i, 0)),
        out_shape=jax.ShapeDtypeStruct((n, d), F32),
        compiler_params=_params("parallel"),
        name="od_out",
    )(x2, y, w)


def _row(v):
    return v.astype(F32).reshape(1, -1)


def _pick(n, pref):
    t = min(n, pref)
    assert n % t == 0, (n, pref)
    return t


def kernel(x, ev_norm_mix, ev_w_in, ev_gmlp_ln_g, ev_gmlp_ln_b, ev_gmlp_ws, ev_gmlp_bs, ev_s5_lambda_re, ev_s5_lambda_im, ev_s5_log_dt, ev_s5_b_re, ev_s5_b_im, ev_s5_c_re, ev_s5_c_im, ev_s5_d, ev_s5_w_glu, ev_s5_b_glu, ev_w_out, ev_norm_ffn, ev_ffn_w_gate, ev_ffn_w_up, ev_ffn_w_down, od_norm_mix, od_w_in, od_q_norm, od_k_norm, od_lambda_q1, od_lambda_k1, od_lambda_q2, od_lambda_k2, od_subln, od_w_out, od_norm_ffn, od_router, od_moe_w_gate, od_moe_w_up, od_moe_w_down):
    bsz, seq, d = x.shape
    n = bsz * seq
    depth = ev_norm_mix.shape[0] + od_norm_mix.shape[0]
    a_width = ev_gmlp_ln_g.shape[-1]
    assert seq % (S5_BLOCK * 2) == 0 and seq % GMLP_CHUNK == 0
    r = seq // S5_BLOCK
    nsteps = max(1, (r - 1).bit_length())
    assert 1 << nsteps >= r

    tm_in = _pick(n, 512)
    tm_ffn = _pick(n, 1024)
    bq = _pick(seq, 512)
    tril = jnp.tril(jnp.ones((GMLP_CHUNK, GMLP_CHUNK), F32))
    seg_w = 2 * ATT_HEAD_DIM * 2
    seg_id = jnp.arange(seg_w) // ATT_HEAD_DIM
    seg = (seg_id[:, None] == seg_id[None, :]).astype(BF16)

    x2 = x.reshape(n, d)
    for layer in range(depth):
        i = layer // 2
        if layer % 2 == 0:
            wm = (ev_gmlp_ws[i].astype(F32) * tril).astype(BF16)
            bias = jnp.repeat(ev_gmlp_bs[i].astype(F32).T, a_width // A_HEADS, axis=1)
            ya, zb = _ev_in(x2, _row(ev_norm_mix[i]), ev_w_in[i].astype(BF16), _row(ev_gmlp_ln_g[i]),
                            _row(ev_gmlp_ln_b[i]), wm, bias, tm=tm_in)
            bw = zb.shape[1]
            zbt = zb.reshape(bsz, r, S5_BLOCK, bw).transpose(0, 2, 3, 1)
            ops = _s5_operators(ev_s5_lambda_re[i], ev_s5_lambda_im[i], ev_s5_log_dt[i], ev_s5_b_re[i],
                                ev_s5_b_im[i], ev_s5_c_re[i], ev_s5_c_im[i], ev_s5_d[i], nsteps)
            ybt = _s5(zbt, *ops)
            yb = ybt.transpose(0, 3, 1, 2).reshape(n, bw)
            w_out = ev_w_out[i].astype(BF16)
            x2 = _ev_out(x2, ya, yb, ev_s5_w_glu[i].astype(BF16), _row(ev_s5_b_glu[i]),
                         w_out[:a_width], w_out[a_width:], tm=tm_in)
            x2 = _ffn(x2, _row(ev_norm_ffn[i]), ev_ffn_w_gate[i:i + 1].astype(BF16),
                      ev_ffn_w_up[i:i + 1].astype(BF16), ev_ffn_w_down[i:i + 1].astype(BF16),
                      tm=tm_ffn, tf=256)
        else:
            lambda_init = 0.8 - 0.6 * math.exp(-0.3 * layer)
            gqk = jnp.concatenate([
                jnp.tile(od_q_norm[i].astype(F32) * (ATT_HEAD_DIM ** -0.5), 2 * ATT_HEADS),
                jnp.tile(od_k_norm[i].astype(F32), 2 * ATT_HEADS)]).reshape(1, -1)
            qkv = _qkv(x2, _row(od_norm_mix[i]), od_w_in[i].astype(BF16), seg, gqk, tm=tm_in)
            lam = (jnp.exp(jnp.sum(od_lambda_q1[i].astype(F32) * od_lambda_k1[i].astype(F32)))
                   - jnp.exp(jnp.sum(od_lambda_q2[i].astype(F32) * od_lambda_k2[i].astype(F32)))
                   + lambda_init).reshape(1)
            yc = _attention(qkv.reshape(bsz, seq, -1), lam, _row(od_subln[i]), bq=bq, bk=bq,
                            lambda_init=lambda_init)
            x2 = _proj_res(x2, yc.reshape(n, -1), od_w_out[i].astype(BF16), tm=tm_in)
            router = jnp.pad(od_router[i].astype(F32), ((0, 0), (0, ROUTER_LANES - N_EXPERTS)))
            x2 = _ffn(x2, _row(od_norm_ffn[i]), od_moe_w_gate[i].astype(BF16), od_moe_w_up[i].astype(BF16),
                      od_moe_w_down[i].astype(BF16), router, tm=tm_ffn, tf=512)
    return x2.reshape(bsz, seq, d)
```

```python
import functools
import math

import numpy as np
import jax
import jax.numpy as jnp
from jax import lax
from jax.experimental import pallas as pl
from jax.experimental.pallas import tpu as pltpu

F32 = jnp.float32
BF16 = jnp.bfloat16
I32 = jnp.int32

EPS = 1e-6
A_HEADS = 4
GMLP_CHUNK = 128
S5_GROUP = 16
S5_STATE = 64
S5_BLOCK = 16
DT_LAMBDA_RE_MAX = -1e-4
ATT_HEADS = 8
ATT_HEAD_DIM = 64
N_EXPERTS = 8
ROUTER_LANES = 128
MOE_TOKEN_TILE = 512
MOE_ROW_TILE = 1024
LOG2E = 1.4426950408889634
VMEM_LIMIT = 56 * 1024 * 1024


def _params(*sem):
    return pltpu.CompilerParams(dimension_semantics=sem, vmem_limit_bytes=VMEM_LIMIT)


def _rms(x, g):
    return x * lax.rsqrt(jnp.mean(x * x, axis=-1, keepdims=True) + EPS) * g


def _gelu(x):
    return 0.5 * x * (1.0 + jnp.tanh(0.7978845608028654 * (x + 0.044715 * (x * x * x))))


def _sigmoid(x):
    return 1.0 / (1.0 + jnp.exp(-x))


def _dot(a, b):
    return jnp.dot(a, b, preferred_element_type=F32)


def _dot_nt(a, b):
    return lax.dot_general(a, b, (((1,), (1,)), ((), ())), preferred_element_type=F32)


def _ev_in_kernel(x_ref, gn_ref, w_ref, lng_ref, lnb_ref, wm_ref, bias_ref, ya_ref, zb_ref, *, n_chunk, a_width):
    h = _rms(x_ref[...], gn_ref[...]).astype(BF16)
    z = _dot(h, w_ref[...])
    zb_ref[...] = z[:, 2 * a_width:].astype(BF16)
    g = _gelu(z[:, :2 * a_width])
    u = g[:, :a_width]
    v = g[:, a_width:]
    mu = jnp.mean(v, axis=-1, keepdims=True)
    vc = v - mu
    vn = vc * lax.rsqrt(jnp.mean(vc * vc, axis=-1, keepdims=True) + EPS) * lng_ref[...] + lnb_ref[...]
    vb = vn.astype(BF16)
    hd = a_width // A_HEADS
    for c in range(n_chunk):
        rows = slice(c * GMLP_CHUNK, (c + 1) * GMLP_CHUNK)
        for k in range(A_HEADS):
            cols = slice(k * hd, (k + 1) * hd)
            s = _dot(wm_ref[k], vb[rows, cols]) + bias_ref[:, cols]
            ya_ref[rows, cols] = (u[rows, cols] * s).astype(BF16)


def _ev_in(x2, gn, w_in, ln_g, ln_b, wm, bias, *, tm):
    n, d = x2.shape
    cols = w_in.shape[1]
    a_width = ln_g.shape[-1]
    b_width = cols - 2 * a_width
    kern = functools.partial(_ev_in_kernel, n_chunk=tm // GMLP_CHUNK, a_width=a_width)
    return pl.pallas_call(
        kern,
        grid=(n // tm,),
        in_specs=[
            pl.BlockSpec((tm, d), lambda i: (i, 0)),
            pl.BlockSpec((1, d), lambda i: (0, 0)),
            pl.BlockSpec((d, cols), lambda i: (0, 0)),
            pl.BlockSpec((1, a_width), lambda i: (0, 0)),
            pl.BlockSpec((1, a_width), lambda i: (0, 0)),
            pl.BlockSpec((A_HEADS, GMLP_CHUNK, GMLP_CHUNK), lambda i: (0, 0, 0)),
            pl.BlockSpec((GMLP_CHUNK, a_width), lambda i: (0, 0)),
        ],
        out_specs=[
            pl.BlockSpec((tm, a_width), lambda i: (i, 0)),
            pl.BlockSpec((tm, b_width), lambda i: (i, 0)),
        ],
        out_shape=[
            jax.ShapeDtypeStruct((n, a_width), BF16),
            jax.ShapeDtypeStruct((n, b_width), BF16),
        ],
        compiler_params=_params("parallel"),
        name="ev_in_gmlp",
    )(x2, gn, w_in, ln_g, ln_b, wm, bias)


def _s5_kernel(u_ref, tt_ref, pt_ref, qt_ref, ar_ref, ai_ref, o_ref, *, nb, nsteps):
    r = u_ref.shape[-1]
    rows = S5_BLOCK * S5_GROUP
    lane = lax.broadcasted_iota(jnp.int32, (S5_STATE, r), 1)

    def shifted(a, sh):
        return jnp.where(lane >= sh, pltpu.roll(a, sh, 1), 0.0)

    for b in range(nb):
        ut = u_ref[b].reshape(rows, r)
        st = _dot(pt_ref[...], ut)
        xr = st[:S5_STATE]
        xi = st[S5_STATE:]
        for j in range(nsteps):
            sh = 1 << j
            ar = ar_ref[j]
            ai = ai_ref[j]
            sr = shifted(xr, sh)
            si = shifted(xi, sh)
            xr, xi = xr + ar * sr - ai * si, xi + ar * si + ai * sr
        xprev = jnp.concatenate([shifted(xr, 1), shifted(xi, 1)], axis=0).astype(BF16)
        y = _dot(tt_ref[...], ut) + _dot(qt_ref[...], xprev)
        o_ref[b] = _gelu(y).astype(BF16).reshape(S5_BLOCK, S5_GROUP, r)


def _s5(zbt, tt, pt, qt, ar, ai):
    bsz, _, width, r = zbt.shape
    groups = width // S5_GROUP
    nsteps = ar.shape[1]
    u5 = zbt.reshape(bsz, S5_BLOCK, groups, S5_GROUP, r)
    rows = S5_BLOCK * S5_GROUP
    kern = functools.partial(_s5_kernel, nb=bsz, nsteps=nsteps)
    out = pl.pallas_call(
        kern,
        grid=(groups,),
        in_specs=[
            pl.BlockSpec((bsz, S5_BLOCK, None, S5_GROUP, r), lambda g: (0, 0, g, 0, 0)),
            pl.BlockSpec((None, rows, rows), lambda g: (g, 0, 0)),
            pl.BlockSpec((None, 2 * S5_STATE, rows), lambda g: (g, 0, 0)),
            pl.BlockSpec((None, rows, 2 * S5_STATE), lambda g: (g, 0, 0)),
            pl.BlockSpec((None, nsteps, S5_STATE, 1), lambda g: (g, 0, 0, 0)),
            pl.BlockSpec((None, nsteps, S5_STATE, 1), lambda g: (g, 0, 0, 0)),
        ],
        out_specs=pl.BlockSpec((bsz, S5_BLOCK, None, S5_GROUP, r), lambda g: (0, 0, g, 0, 0)),
        out_shape=jax.ShapeDtypeStruct(u5.shape, BF16),
        compiler_params=_params("parallel"),
        name="ev_s5",
    )(u5, tt, pt, qt, ar, ai)
    return out.reshape(zbt.shape)


def _s5_operators(lam_re, lam_im, log_dt, b_re, b_im, c_re, c_im, d, nsteps):
    groups, p = lam_re.shape
    c = S5_GROUP
    ln = S5_BLOCK
    lr = jnp.minimum(lam_re.astype(F32), DT_LAMBDA_RE_MAX)
    li = lam_im.astype(F32)
    dt = jnp.exp(log_dt.astype(F32))[:, None]
    zr = lr * dt
    zi = li * dt

    def powers(k):
        kk = jnp.asarray(k, F32)[None, :, None]
        mag = jnp.exp(zr[:, None, :] * kk)
        ang = zi[:, None, :] * kk
        return mag * jnp.cos(ang), mag * jnp.sin(ang)

    lbr, lbi = powers(jnp.ones((1,)))
    lbr = lbr[:, 0]
    lbi = lbi[:, 0]
    den = lr * lr + li * li
    fr = ((lbr - 1.0) * lr + lbi * li) / den
    fi = (lbi * lr - (lbr - 1.0) * li) / den
    bbr = fr[..., None] * b_re.astype(F32) - fi[..., None] * b_im.astype(F32)
    bbi = fr[..., None] * b_im.astype(F32) + fi[..., None] * b_re.astype(F32)
    cr = c_re.astype(F32)
    ci = c_im.astype(F32)

    pwr, pwi = powers(jnp.arange(ln + 1))
    mr = pwr[:, :ln, :, None] * bbr[:, None] - pwi[:, :ln, :, None] * bbi[:, None]
    mi = pwr[:, :ln, :, None] * bbi[:, None] + pwi[:, :ln, :, None] * bbr[:, None]
    kmat = jnp.einsum("gcp,gkpd->gkcd", cr, mr) - jnp.einsum("gcp,gkpd->gkcd", ci, mi)
    s_idx = jnp.arange(ln)[:, None]
    t_idx = jnp.arange(ln)[None, :]
    lag = t_idx - s_idx
    blocks = kmat[:, jnp.clip(lag, 0, ln - 1)]
    blocks = jnp.where((lag >= 0)[None, :, :, None, None], blocks, 0.0)
    skip = (jnp.eye(ln, dtype=F32)[None, :, :, None, None]
            * jnp.eye(c, dtype=F32)[None, None, None]
            * d.astype(F32).reshape(groups, 1, 1, c, 1))
    blocks = blocks + skip
    tt = blocks.transpose(0, 2, 3, 1, 4).reshape(groups, ln * c, ln * c)

    rev = ln - 1 - jnp.arange(ln)
    pr = pwr[:, rev][:, :, :, None] * bbr[:, None] - pwi[:, rev][:, :, :, None] * bbi[:, None]
    pi = pwr[:, rev][:, :, :, None] * bbi[:, None] + pwi[:, rev][:, :, :, None] * bbr[:, None]
    pt = jnp.concatenate([pr, pi], axis=2).transpose(0, 2, 1, 3).reshape(groups, 2 * p, ln * c)

    a_r = cr[:, None] * pwr[:, 1:ln + 1, None, :] - ci[:, None] * pwi[:, 1:ln + 1, None, :]
    a_i = cr[:, None] * pwi[:, 1:ln + 1, None, :] + ci[:, None] * pwr[:, 1:ln + 1, None, :]
    qt = jnp.concatenate([a_r, -a_i], axis=-1).reshape(groups, ln * c, 2 * p)

    sr, si = powers(ln * (2 ** jnp.arange(nsteps)))
    return (tt.astype(BF16), pt.astype(BF16), qt.astype(BF16), sr[..., None], si[..., None])


def _ev_out_kernel(x_ref, ya_ref, yb_ref, wglu_ref, bglu_ref, woa_ref, wob_ref, o_ref):
    yb = yb_ref[...]
    gate = _sigmoid(_dot(yb, wglu_ref[...]) + bglu_ref[...])
    y2 = (yb.astype(F32) * gate).astype(BF16)
    o_ref[...] = x_ref[...] + _dot(ya_ref[...], woa_ref[...]) + _dot(y2, wob_ref[...])


def _ev_out(x2, ya, yb, wglu, bglu, woa, wob, *, tm):
    n, d = x2.shape
    aw = ya.shape[1]
    bw = yb.shape[1]
    return pl.pallas_call(
        _ev_out_kernel,
        grid=(n // tm,),
        in_specs=[
            pl.BlockSpec((tm, d), lambda i: (i, 0)),
            pl.BlockSpec((tm, aw), lambda i: (i, 0)),
            pl.BlockSpec((tm, bw), lambda i: (i, 0)),
            pl.BlockSpec((bw, bw), lambda i: (0, 0)),
            pl.BlockSpec((1, bw), lambda i: (0, 0)),
            pl.BlockSpec((aw, d), lambda i: (0, 0)),
            pl.BlockSpec((bw, d), lambda i: (0, 0)),
        ],
        out_specs=pl.BlockSpec((tm, d), lambda i: (i, 0)),
        out_shape=jax.ShapeDtypeStruct((n, d), F32),
        compiler_params=_params("parallel"),
        name="ev_out",
    )(x2, ya, yb, wglu, bglu, woa, wob)


def _ffn_kernel(x_ref, gn_ref, wg_ref, wu_ref, wd_ref, o_ref, hn_ref, acc_ref):
    f = pl.program_id(1)

    @pl.when(f == 0)
    def _():
        hn_ref[...] = _rms(x_ref[...], gn_ref[...]).astype(BF16)
        acc_ref[...] = jnp.zeros_like(acc_ref)

    h = hn_ref[...]
    a = _dot(h, wg_ref[...])
    b = _dot(h, wu_ref[...])
    acc_ref[...] += _dot((a * _sigmoid(a) * b).astype(BF16), wd_ref[...])

    @pl.when(f == pl.num_programs(1) - 1)
    def _():
        o_ref[...] = x_ref[...] + acc_ref[...]


def _ffn(x2, gn, wg, wu, wd, *, tm, tf):
    n, d = x2.shape
    dff = wg.shape[1]
    return pl.pallas_call(
        _ffn_kernel,
        grid=(n // tm, dff // tf),
        in_specs=[
            pl.BlockSpec((tm, d), lambda i, f: (i, 0)),
            pl.BlockSpec((1, d), lambda i, f: (0, 0)),
            pl.BlockSpec((d, tf), lambda i, f: (0, f)),
            pl.BlockSpec((d, tf), lambda i, f: (0, f)),
            pl.BlockSpec((tf, d), lambda i, f: (f, 0)),
        ],
        out_specs=pl.BlockSpec((tm, d), lambda i, f: (i, 0)),
        out_shape=jax.ShapeDtypeStruct((n, d), F32),
        scratch_shapes=[pltpu.VMEM((tm, d), BF16), pltpu.VMEM((tm, d), F32)],
        compiler_params=_params("parallel", "arbitrary"),
        name="dense_ffn",
    )(x2, gn, wg, wu, wd)


def _route_kernel(x_ref, gn_ref, router_ref, tri_ref, hn_ref, meta_ref, cnt_ref, carry_ref):
    @pl.when(pl.program_id(0) == 0)
    def _():
        carry_ref[...] = jnp.zeros_like(carry_ref)

    hn = _rms(x_ref[...], gn_ref[...])
    hn_ref[...] = hn.astype(BF16)
    logits = jnp.dot(hn, router_ref[...], preferred_element_type=F32, precision=lax.Precision.HIGHEST)
    lane = lax.broadcasted_iota(I32, logits.shape, 1)
    neg = jnp.float32(-jnp.inf)
    logits = jnp.where(lane < N_EXPERTS, logits, neg)
    m1 = jnp.max(logits, axis=-1, keepdims=True)
    i1 = jnp.min(jnp.where(logits == m1, lane, ROUTER_LANES), axis=-1, keepdims=True)
    rest = jnp.where(lane == i1, neg, logits)
    m2 = jnp.max(rest, axis=-1, keepdims=True)
    i2 = jnp.min(jnp.where(rest == m2, lane, ROUTER_LANES), axis=-1, keepdims=True)
    e2 = jnp.exp(m2 - m1)
    w1 = 1.0 / (1.0 + e2)
    w2 = e2 / (1.0 + e2)
    chosen = jnp.where(lane == i1, 1.0, jnp.where(lane == i2, 1.0, 0.0))
    before = _dot(tri_ref[...], chosen.astype(BF16)) + carry_ref[...]
    r1 = jnp.sum(jnp.where(lane == i1, before, 0.0), axis=-1, keepdims=True)
    r2 = jnp.sum(jnp.where(lane == i2, before, 0.0), axis=-1, keepdims=True)
    cnt = jnp.sum(chosen, axis=0, keepdims=True)
    cnt_ref[...] = cnt
    carry_ref[...] += cnt
    fields = (i1.astype(F32), i2.astype(F32), r1, r2, w1, w2)
    meta = jnp.zeros_like(logits)
    for k, val in enumerate(fields):
        meta = jnp.where(lane == k, val, meta)
    meta_ref[...] = meta


def _route(x2, gn, router, *, tm):
    n, d = x2.shape
    nt = n // tm
    tri = jnp.tril(jnp.ones((tm, tm), F32), -1).astype(BF16)
    return pl.pallas_call(
        _route_kernel,
        grid=(nt,),
        in_specs=[
            pl.BlockSpec((tm, d), lambda i: (i, 0)),
            pl.BlockSpec((1, d), lambda i: (0, 0)),
            pl.BlockSpec((d, ROUTER_LANES), lambda i: (0, 0)),
            pl.BlockSpec((tm, tm), lambda i: (0, 0)),
        ],
        out_specs=[
            pl.BlockSpec((tm, d), lambda i: (i, 0)),
            pl.BlockSpec((tm, ROUTER_LANES), lambda i: (i, 0)),
            pl.BlockSpec((None, 1, ROUTER_LANES), lambda i: (i, 0, 0)),
        ],
        out_shape=[
            jax.ShapeDtypeStruct((n, d), BF16),
            jax.ShapeDtypeStruct((n, ROUTER_LANES), F32),
            jax.ShapeDtypeStruct((nt, 1, ROUTER_LANES), F32),
        ],
        scratch_shapes=[pltpu.VMEM((1, ROUTER_LANES), F32)],
        compiler_params=_params("arbitrary"),
        name="moe_route",
    )(x2, gn, router, tri)


def _moe_tables(counts, n_row_tiles):
    nt = counts.shape[0]
    t = MOE_TOKEN_TILE
    tf = MOE_ROW_TILE
    tot = counts.sum(0)
    padded = ((tot + tf - 1) // tf) * tf
    off = jnp.cumsum(padded) - padded
    before = jnp.cumsum(counts, axis=0) - counts
    start = off[None] + before
    last = start + jnp.maximum(counts, 1) - 1
    j0 = start // t
    j1 = last // t
    v0 = counts > 0
    v1 = v0 & (j1 != j0)
    ti = jnp.broadcast_to(jnp.arange(nt, dtype=I32)[:, None], counts.shape)
    cand_i = jnp.concatenate([ti.ravel(), ti.ravel()])
    cand_j = jnp.concatenate([j0.ravel(), j1.ravel()]).astype(I32)
    valid = jnp.concatenate([v0.ravel(), v1.ravel()])
    npairs = valid.sum().astype(I32)
    n_cand = cand_i.shape[0]
    clamp = jnp.minimum(jnp.arange(n_cand), npairs - 1)
    big = jnp.iinfo(jnp.int32).max

    def ordered(key):
        order = jnp.argsort(jnp.where(valid, key, big))[clamp]
        return cand_i[order], cand_j[order]

    n_disp_tiles = n_row_tiles * (tf // t)
    disp_i, disp_j = ordered(cand_j * nt + cand_i)
    comb_i, comb_j = ordered(cand_i * n_disp_tiles + cand_j)
    tile_start = jnp.arange(n_row_tiles, dtype=I32) * tf
    ends = (off + padded).astype(I32)
    tile_valid = (tile_start < ends[-1]).astype(I32)
    tile_expert = jnp.sum(tile_start[:, None] >= ends[None, :], axis=1).astype(I32)
    last_expert = jnp.max(jnp.where(tot > 0, jnp.arange(N_EXPERTS), 0)).astype(I32)
    tile_expert = jnp.where(tile_valid > 0, tile_expert, last_expert)
    return off.astype(I32), npairs.reshape(1), disp_i, disp_j, comb_i, comb_j, tile_expert, tile_valid


def _dispatch_kernel(np_ref, ti_ref, tj_ref, hn_ref, p1_ref, p2_ref, xs_in_ref, xs_ref):
    del xs_in_ref
    p = pl.program_id(0)
    j = tj_ref[p]
    first = (p == 0) | (tj_ref[jnp.maximum(p - 1, 0)] != j)

    @pl.when(p < np_ref[0])
    def _():
        t = xs_ref.shape[0]
        row = j * t + lax.broadcasted_iota(I32, (t, p1_ref.shape[-1]), 0)
        sel = jnp.where(row == p1_ref[...], 1.0, jnp.where(row == p2_ref[...], 1.0, 0.0))
        rows = _dot(sel.astype(BF16), hn_ref[...]).astype(BF16)

        @pl.when(first)
        def _():
            xs_ref[...] = rows

        @pl.when(jnp.logical_not(first))
        def _():
            xs_ref[...] += rows


def _dispatch(npairs, disp_i, disp_j, hn, pos1_rows, pos2_rows, n_rows):
    n, d = hn.shape
    t = MOE_TOKEN_TILE
    xs0 = jnp.zeros((n_rows, d), BF16)
    grid_spec = pltpu.PrefetchScalarGridSpec(
        num_scalar_prefetch=3,
        grid=(disp_i.shape[0],),
        in_specs=[
            pl.BlockSpec((t, d), lambda p, np_, ti, tj: (ti[p], 0)),
            pl.BlockSpec((None, 1, t), lambda p, np_, ti, tj: (ti[p], 0, 0)),
            pl.BlockSpec((None, 1, t), lambda p, np_, ti, tj: (ti[p], 0, 0)),
            pl.BlockSpec(memory_space=pl.ANY),
        ],
        out_specs=pl.BlockSpec((t, d), lambda p, np_, ti, tj: (tj[p], 0)),
    )
    return pl.pallas_call(
        _dispatch_kernel,
        grid_spec=grid_spec,
        out_shape=jax.ShapeDtypeStruct((n_rows, d), BF16),
        input_output_aliases={6: 0},
        compiler_params=_params("arbitrary"),
        name="moe_dispatch",
    )(npairs, disp_i, disp_j, hn, pos1_rows, pos2_rows, xs0)


def _experts_kernel(te_ref, tv_ref, xs_ref, wg_ref, wu_ref, wd_ref, ys_ref, acc_ref):
    del te_ref
    j = pl.program_id(0)
    f = pl.program_id(1)
    valid = tv_ref[j] > 0

    @pl.when(f == 0)
    def _():
        acc_ref[...] = jnp.zeros_like(acc_ref)

    @pl.when(valid)
    def _():
        h = xs_ref[...]
        a = _dot(h, wg_ref[...])
        b = _dot(h, wu_ref[...])
        acc_ref[...] += _dot((a * _sigmoid(a) * b).astype(BF16), wd_ref[...])

    @pl.when(f == pl.num_programs(1) - 1)
    def _():
        ys_ref[...] = acc_ref[...].astype(BF16)


def _experts(tile_expert, tile_valid, xs, wg, wu, wd, *, tf):
    n_rows, d = xs.shape
    dff = wg.shape[2]
    tm = MOE_ROW_TILE
    grid_spec = pltpu.PrefetchScalarGridSpec(
        num_scalar_prefetch=2,
        grid=(n_rows // tm, dff // tf),
        in_specs=[
            pl.BlockSpec((tm, d), lambda j, f, te, tv: (j, 0)),
            pl.BlockSpec((None, d, tf), lambda j, f, te, tv: (te[j], 0, f * tv[j])),
            pl.BlockSpec((None, d, tf), lambda j, f, te, tv: (te[j], 0, f * tv[j])),
            pl.BlockSpec((None, tf, d), lambda j, f, te, tv: (te[j], f * tv[j], 0)),
        ],
        out_specs=pl.BlockSpec((tm, d), lambda j, f, te, tv: (j, 0)),
        scratch_shapes=[pltpu.VMEM((tm, d), F32)],
    )
    return pl.pallas_call(
        _experts_kernel,
        grid_spec=grid_spec,
        out_shape=jax.ShapeDtypeStruct((n_rows, d), BF16),
        compiler_params=_params("parallel", "arbitrary"),
        name="moe_experts",
    )(tile_expert, tile_valid, xs, wg, wu, wd)


def _combine_kernel(np_ref, ti_ref, tj_ref, x_ref, ys_ref, p1_ref, p2_ref, w1_ref, w2_ref, o_ref):
    p = pl.program_id(0)
    i = ti_ref[p]
    first = (p == 0) | (ti_ref[jnp.maximum(p - 1, 0)] != i)

    @pl.when(p < np_ref[0])
    def _():
        t = ys_ref.shape[0]
        col = tj_ref[p] * t + lax.broadcasted_iota(I32, (x_ref.shape[0], t), 1)
        sel = jnp.where(col == p1_ref[...], w1_ref[...], jnp.where(col == p2_ref[...], w2_ref[...], 0.0))
        upd = _dot(sel.astype(BF16), ys_ref[...])

        @pl.when(first)
        def _():
            o_ref[...] = x_ref[...] + upd

        @pl.when(jnp.logical_not(first))
        def _():
            o_ref[...] += upd


def _combine(npairs, comb_i, comb_j, x2, ys, pos1, pos2, w1, w2):
    n, d = x2.shape
    t = MOE_TOKEN_TILE
    tok = lambda p, np_, ti, tj: (ti[p], 0)
    grid_spec = pltpu.PrefetchScalarGridSpec(
        num_scalar_prefetch=3,
        grid=(comb_i.shape[0],),
        in_specs=[
            pl.BlockSpec((t, d), tok),
            pl.BlockSpec((t, d), lambda p, np_, ti, tj: (tj[p], 0)),
            pl.BlockSpec((t, 1), tok),
            pl.BlockSpec((t, 1), tok),
            pl.BlockSpec((t, 1), tok),
            pl.BlockSpec((t, 1), tok),
        ],
        out_specs=pl.BlockSpec((t, d), tok),
    )
    return pl.pallas_call(
        _combine_kernel,
        grid_spec=grid_spec,
        out_shape=jax.ShapeDtypeStruct((n, d), F32),
        compiler_params=_params("arbitrary"),
        name="moe_combine",
    )(npairs, comb_i, comb_j, x2, ys, pos1, pos2, w1, w2)


def _moe(x2, gn, router, wg, wu, wd):
    n, d = x2.shape
    t = MOE_TOKEN_TILE
    nt = n // t
    n_row_tiles = (2 * n + N_EXPERTS * (MOE_ROW_TILE - 1)) // MOE_ROW_TILE
    hn, meta, cnt = _route(x2, gn, router, tm=t)
    counts = cnt[:, 0, :N_EXPERTS].astype(I32)
    off, npairs, disp_i, disp_j, comb_i, comb_j, tile_expert, tile_valid = _moe_tables(counts, n_row_tiles)
    i1 = meta[:, 0].astype(I32)
    i2 = meta[:, 1].astype(I32)
    pos1 = off[i1] + meta[:, 2].astype(I32)
    pos2 = off[i2] + meta[:, 3].astype(I32)
    xs = _dispatch(npairs, disp_i, disp_j, hn, pos1.reshape(nt, 1, t), pos2.reshape(nt, 1, t),
                   n_row_tiles * MOE_ROW_TILE)
    ys = _experts(tile_expert, tile_valid, xs, wg, wu, wd, tf=512)
    return _combine(npairs, comb_i, comb_j, x2, ys, pos1.reshape(n, 1), pos2.reshape(n, 1),
                    meta[:, 4:5], meta[:, 5:6])


def _qkv_kernel(x_ref, gn_ref, wqk_ref, wvt_ref, seg_ref, gqk_ref, qk_ref, vt_ref):
    h = _rms(x_ref[...], gn_ref[...]).astype(BF16)
    vt_ref[...] = _dot_nt(wvt_ref[...], h).astype(BF16)
    z = _dot(h, wqk_ref[...])
    seg = seg_ref[...]
    sw = seg.shape[0]
    for c in range(z.shape[1] // sw):
        cols = slice(c * sw, (c + 1) * sw)
        zc = z[:, cols]
        ms = _dot((zc * zc).astype(BF16), seg) * (1.0 / ATT_HEAD_DIM)
        qk_ref[:, cols] = (zc * lax.rsqrt(ms + EPS) * gqk_ref[:, cols]).astype(BF16)


def _qkv(x2, gn, wqk, wvt, seg, gqk, *, bsz, tm):
    n, d = x2.shape
    seq = n // bsz
    qk_cols = wqk.shape[1]
    v_cols = wvt.shape[0]
    per_b = seq // tm
    return pl.pallas_call(
        _qkv_kernel,
        grid=(bsz, per_b),
        in_specs=[
            pl.BlockSpec((tm, d), lambda b, i: (b * per_b + i, 0)),
            pl.BlockSpec((1, d), lambda b, i: (0, 0)),
            pl.BlockSpec((d, qk_cols), lambda b, i: (0, 0)),
            pl.BlockSpec((v_cols, d), lambda b, i: (0, 0)),
            pl.BlockSpec(seg.shape, lambda b, i: (0, 0)),
            pl.BlockSpec((1, qk_cols), lambda b, i: (0, 0)),
        ],
        out_specs=[
            pl.BlockSpec((tm, qk_cols), lambda b, i: (b * per_b + i, 0)),
            pl.BlockSpec((None, v_cols, tm), lambda b, i: (b, 0, i)),
        ],
        out_shape=[
            jax.ShapeDtypeStruct((n, qk_cols), BF16),
            jax.ShapeDtypeStruct((bsz, v_cols, seq), BF16),
        ],
        compiler_params=_params("parallel", "parallel"),
        name="od_qkv",
    )(x2, gn, wqk, wvt, seg, gqk)


def _attn_kernel(qi_ref, ki_ref, lam_ref, q_ref, k_ref, vt_ref, sg_ref, o_ref, qm_ref, m_ref, l_ref, acc_ref,
                 *, bq, bk, out_scale):
    p = pl.program_id(2)
    qi = qi_ref[p]
    ki = ki_ref[p]
    last_k = ((qi + 1) * bq - 1) // bk

    @pl.when(ki == 0)
    def _():
        q = q_ref[...]
        lane = lax.broadcasted_iota(I32, q.shape, 1)
        zero = jnp.zeros_like(q)
        qm_ref[0] = jnp.where(lane < ATT_HEAD_DIM, q, zero)
        qm_ref[1] = jnp.where(lane >= ATT_HEAD_DIM, q, zero)
        m_ref[...] = jnp.full_like(m_ref, -jnp.inf)
        l_ref[...] = jnp.zeros_like(l_ref)
        acc_ref[...] = jnp.zeros_like(acc_ref)

    def step(masked):
        k = k_ref[...]
        vt = vt_ref[...]
        if masked:
            kpos = ki * bk + lax.broadcasted_iota(I32, (bk, bq), 0)
            qpos = qi * bq + lax.broadcasted_iota(I32, (bk, bq), 1)
            keep = kpos <= qpos
        for m in range(2):
            s = _dot_nt(k, qm_ref[m])
            if masked:
                s = jnp.where(keep, s, -jnp.inf)
            m_old = m_ref[m]
            m_new = jnp.maximum(m_old, jnp.max(s, axis=0, keepdims=True))
            pt = jnp.exp2(s - m_new)
            alpha = jnp.exp2(m_old - m_new)
            l_ref[m] = alpha * l_ref[m] + jnp.sum(pt, axis=0, keepdims=True)
            acc_ref[m] = alpha * acc_ref[m] + _dot(vt, pt.astype(BF16))
            m_ref[m] = m_new

    fully_visible = (ki + 1) * bk - 1 <= qi * bq

    @pl.when(fully_visible)
    def _():
        step(False)

    @pl.when(jnp.logical_not(fully_visible))
    def _():
        step(True)

    @pl.when(ki == last_k)
    def _():
        o = acc_ref[0] / l_ref[0] - lam_ref[0] * (acc_ref[1] / l_ref[1])
        ms = jnp.mean(o * o, axis=0, keepdims=True)
        o = o * lax.rsqrt(ms + EPS) * (sg_ref[...] * out_scale)
        o_ref[...] = o.T.astype(BF16)


def _attention(qk3, vt3, lam, subln_col, *, bq, bk, lambda_init):
    bsz, seq, _ = qk3.shape
    dv = 2 * ATT_HEAD_DIM
    nh = ATT_HEADS
    pairs = [(qi, ki) for qi in range(seq // bq) for ki in range(((qi + 1) * bq - 1) // bk + 1)]
    qi_tab = jnp.asarray(np.array([p[0] for p in pairs], np.int32))
    ki_tab = jnp.asarray(np.array([p[1] for p in pairs], np.int32))
    kern = functools.partial(_attn_kernel, bq=bq, bk=bk, out_scale=1.0 - lambda_init)
    grid_spec = pltpu.PrefetchScalarGridSpec(
        num_scalar_prefetch=2,
        grid=(bsz, nh, len(pairs)),
        in_specs=[
            pl.BlockSpec(memory_space=pltpu.SMEM),
            pl.BlockSpec((None, bq, dv), lambda b, h, p, qt, kt: (b, qt[p], h)),
            pl.BlockSpec((None, bk, dv), lambda b, h, p, qt, kt: (b, kt[p], nh + h)),
            pl.BlockSpec((None, dv, bk), lambda b, h, p, qt, kt: (b, h, kt[p])),
            pl.BlockSpec((dv, 1), lambda b, h, p, qt, kt: (0, 0)),
        ],
        out_specs=pl.BlockSpec((None, bq, dv), lambda b, h, p, qt, kt: (b, qt[p], h)),
        scratch_shapes=[
            pltpu.VMEM((2, bq, dv), BF16),
            pltpu.VMEM((2, 1, bq), F32),
            pltpu.VMEM((2, 1, bq), F32),
            pltpu.VMEM((2, dv, bq), F32),
        ],
    )
    return pl.pallas_call(
        kern,
        grid_spec=grid_spec,
        out_shape=jax.ShapeDtypeStruct((bsz, seq, nh * dv), BF16),
        compiler_params=_params("parallel", "parallel", "arbitrary"),
        name="od_attn",
    )(qi_tab, ki_tab, lam, qk3, qk3, vt3, subln_col)


def _proj_res_kernel(x_ref, y_ref, w_ref, o_ref):
    o_ref[...] = x_ref[...] + _dot(y_ref[...], w_ref[...])


def _proj_res(x2, y, w, *, tm):
    n, d = x2.shape
    k = y.shape[1]
    return pl.pallas_call(
        _proj_res_kernel,
        grid=(n // tm,),
        in_specs=[
            pl.BlockSpec((tm, d), lambda i: (i, 0)),
            pl.BlockSpec((tm, k), lambda i: (i, 0)),
            pl.BlockSpec((k, d), lambda i: (0, 0)),
        ],
        out_specs=pl.BlockSpec((tm, d), lambda i: (i, 0)),
        out_shape=jax.ShapeDtypeStruct((n, d), F32),
        compiler_params=_params("parallel"),
        name="od_out",
    )(x2, y, w)


def _row(v):
    return v.astype(F32).reshape(1, -1)


def _pick(n, pref):
    t = min(n, pref)
    assert n % t == 0, (n, pref)
    return t


def kernel(x, ev_norm_mix, ev_w_in, ev_gmlp_ln_g, ev_gmlp_ln_b, ev_gmlp_ws, ev_gmlp_bs, ev_s5_lambda_re, ev_s5_lambda_im, ev_s5_log_dt, ev_s5_b_re, ev_s5_b_im, ev_s5_c_re, ev_s5_c_im, ev_s5_d, ev_s5_w_glu, ev_s5_b_glu, ev_w_out, ev_norm_ffn, ev_ffn_w_gate, ev_ffn_w_up, ev_ffn_w_down, od_norm_mix, od_w_in, od_q_norm, od_k_norm, od_lambda_q1, od_lambda_k1, od_lambda_q2, od_lambda_k2, od_subln, od_w_out, od_norm_ffn, od_router, od_moe_w_gate, od_moe_w_up, od_moe_w_down):
    bsz, seq, d = x.shape
    n = bsz * seq
    depth = ev_norm_mix.shape[0] + od_norm_mix.shape[0]
    a_width = ev_gmlp_ln_g.shape[-1]
    assert seq % (S5_BLOCK * 2) == 0 and seq % GMLP_CHUNK == 0 and n % MOE_TOKEN_TILE == 0
    r = seq // S5_BLOCK
    nsteps = max(1, (r - 1).bit_length())

    tm_in = _pick(seq, 512)
    tm_ffn = _pick(n, 1024)
    bq = _pick(seq, 512)
    qk_cols = 2 * ATT_HEADS * 2 * ATT_HEAD_DIM
    tril = jnp.tril(jnp.ones((GMLP_CHUNK, GMLP_CHUNK), F32))
    seg_id = jnp.arange(4 * ATT_HEAD_DIM) // ATT_HEAD_DIM
    seg = (seg_id[:, None] == seg_id[None, :]).astype(BF16)

    x2 = x.reshape(n, d)
    for layer in range(depth):
        i = layer // 2
        if layer % 2 == 0:
            wm = (ev_gmlp_ws[i].astype(F32) * tril).astype(BF16)
            bias = jnp.repeat(ev_gmlp_bs[i].astype(F32).T, a_width // A_HEADS, axis=1)
            ya, zb = _ev_in(x2, _row(ev_norm_mix[i]), ev_w_in[i].astype(BF16), _row(ev_gmlp_ln_g[i]),
                            _row(ev_gmlp_ln_b[i]), wm, bias, tm=tm_in)
            bw = zb.shape[1]
            zbt = zb.reshape(bsz, r, S5_BLOCK, bw).transpose(0, 2, 3, 1)
            ops = _s5_operators(ev_s5_lambda_re[i], ev_s5_lambda_im[i], ev_s5_log_dt[i], ev_s5_b_re[i],
                                ev_s5_b_im[i], ev_s5_c_re[i], ev_s5_c_im[i], ev_s5_d[i], nsteps)
            ybt = _s5(zbt, *ops)
            yb = ybt.transpose(0, 3, 1, 2).reshape(n, bw)
            w_out = ev_w_out[i].astype(BF16)
            x2 = _ev_out(x2, ya, yb, ev_s5_w_glu[i].astype(BF16), _row(ev_s5_b_glu[i]),
                         w_out[:a_width], w_out[a_width:], tm=tm_in)
            x2 = _ffn(x2, _row(ev_norm_ffn[i]), ev_ffn_w_gate[i].astype(BF16), ev_ffn_w_up[i].astype(BF16),
                      ev_ffn_w_down[i].astype(BF16), tm=tm_ffn, tf=256)
        else:
            lambda_init = 0.8 - 0.6 * math.exp(-0.3 * layer)
            gqk = jnp.concatenate([
                jnp.tile(od_q_norm[i].astype(F32) * (ATT_HEAD_DIM ** -0.5 * LOG2E), 2 * ATT_HEADS),
                jnp.tile(od_k_norm[i].astype(F32), 2 * ATT_HEADS)]).reshape(1, -1)
            w_in = od_w_in[i].astype(BF16)
            qk, vt = _qkv(x2, _row(od_norm_mix[i]), w_in[:, :qk_cols], w_in[:, qk_cols:].T, seg, gqk,
                          bsz=bsz, tm=tm_in)
            lam = (jnp.exp(jnp.sum(od_lambda_q1[i].astype(F32) * od_lambda_k1[i].astype(F32)))
                   - jnp.exp(jnp.sum(od_lambda_q2[i].astype(F32) * od_lambda_k2[i].astype(F32)))
                   + lambda_init).reshape(1)
            yc = _attention(qk.reshape(bsz, seq, qk_cols), vt, lam, od_subln[i].astype(F32).reshape(-1, 1),
                            bq=bq, bk=bq, lambda_init=lambda_init)
            x2 = _proj_res(x2, yc.reshape(n, -1), od_w_out[i].astype(BF16), tm=tm_in)
            router = jnp.pad(od_router[i].astype(F32), ((0, 0), (0, ROUTER_LANES - N_EXPERTS)))
            x2 = _moe(x2, _row(od_norm_ffn[i]), router, od_moe_w_gate[i].astype(BF16),
                      od_moe_w_up[i].astype(BF16), od_moe_w_down[i].astype(BF16))
    return x2.reshape(bsz, seq, d)
```

```python
import functools
import math

import numpy as np
import jax
import jax.numpy as jnp
from jax import lax
from jax.experimental import pallas as pl
from jax.experimental.pallas import tpu as pltpu

F32 = jnp.float32
BF16 = jnp.bfloat16
I32 = jnp.int32

EPS = 1e-6
A_HEADS = 4
GMLP_CHUNK = 128
S5_GROUP = 16
S5_STATE = 64
S5_BLOCK = 16
DT_LAMBDA_RE_MAX = -1e-4
ATT_HEADS = 8
ATT_HEAD_DIM = 64
N_EXPERTS = 8
ROUTER_LANES = 128
MOE_TOKEN_TILE = 512
MOE_ROW_TILE = 1024
LOG2E = 1.4426950408889634
ATT_STATIC_MAX_LIMIT = 40.0
VMEM_LIMIT = 56 * 1024 * 1024


def _params(*sem):
    return pltpu.CompilerParams(dimension_semantics=sem, vmem_limit_bytes=VMEM_LIMIT)


def _rms(x, g):
    return x * lax.rsqrt(jnp.mean(x * x, axis=-1, keepdims=True) + EPS) * g


def _gelu(x):
    return 0.5 * x * (1.0 + jnp.tanh(0.7978845608028654 * (x + 0.044715 * (x * x * x))))


def _sigmoid(x):
    return 1.0 / (1.0 + jnp.exp(-x))


def _dot(a, b):
    return jnp.dot(a, b, preferred_element_type=F32)


def _dot_nt(a, b):
    return lax.dot_general(a, b, (((1,), (1,)), ((), ())), preferred_element_type=F32)


def _ev_in_kernel(x_ref, gn_ref, w_ref, lng_ref, lnb_ref, wm_ref, bias_ref, ya_ref, zb_ref, *, n_chunk, a_width):
    h = _rms(x_ref[...], gn_ref[...]).astype(BF16)
    z = _dot(h, w_ref[...])
    zb_ref[...] = z[:, 2 * a_width:].astype(BF16)
    g = _gelu(z[:, :2 * a_width])
    u = g[:, :a_width]
    v = g[:, a_width:]
    mu = jnp.mean(v, axis=-1, keepdims=True)
    vc = v - mu
    vn = vc * lax.rsqrt(jnp.mean(vc * vc, axis=-1, keepdims=True) + EPS) * lng_ref[...] + lnb_ref[...]
    vb = vn.astype(BF16)
    hd = a_width // A_HEADS
    for c in range(n_chunk):
        rows = slice(c * GMLP_CHUNK, (c + 1) * GMLP_CHUNK)
        for k in range(A_HEADS):
            cols = slice(k * hd, (k + 1) * hd)
            s = _dot(wm_ref[k], vb[rows, cols]) + bias_ref[:, cols]
            ya_ref[rows, cols] = (u[rows, cols] * s).astype(BF16)


def _ev_in(x2, gn, w_in, ln_g, ln_b, wm, bias, *, tm):
    n, d = x2.shape
    cols = w_in.shape[1]
    a_width = ln_g.shape[-1]
    b_width = cols - 2 * a_width
    kern = functools.partial(_ev_in_kernel, n_chunk=tm // GMLP_CHUNK, a_width=a_width)
    return pl.pallas_call(
        kern,
        grid=(n // tm,),
        in_specs=[
            pl.BlockSpec((tm, d), lambda i: (i, 0)),
            pl.BlockSpec((1, d), lambda i: (0, 0)),
            pl.BlockSpec((d, cols), lambda i: (0, 0)),
            pl.BlockSpec((1, a_width), lambda i: (0, 0)),
            pl.BlockSpec((1, a_width), lambda i: (0, 0)),
            pl.BlockSpec((A_HEADS, GMLP_CHUNK, GMLP_CHUNK), lambda i: (0, 0, 0)),
            pl.BlockSpec((GMLP_CHUNK, a_width), lambda i: (0, 0)),
        ],
        out_specs=[
            pl.BlockSpec((tm, a_width), lambda i: (i, 0)),
            pl.BlockSpec((tm, b_width), lambda i: (i, 0)),
        ],
        out_shape=[
            jax.ShapeDtypeStruct((n, a_width), BF16),
            jax.ShapeDtypeStruct((n, b_width), BF16),
        ],
        compiler_params=_params("parallel"),
        name="ev_in_gmlp",
    )(x2, gn, w_in, ln_g, ln_b, wm, bias)


def _s5_kernel(u_ref, tt_ref, pt_ref, qt_ref, ar_ref, ai_ref, o_ref, *, nb, nsteps):
    r = u_ref.shape[-1]
    rows = S5_BLOCK * S5_GROUP
    lane = lax.broadcasted_iota(jnp.int32, (S5_STATE, r), 1)

    def shifted(a, sh):
        return jnp.where(lane >= sh, pltpu.roll(a, sh, 1), 0.0)

    for b in range(nb):
        ut = u_ref[b].reshape(rows, r)
        st = _dot(pt_ref[...], ut)
        xr = st[:S5_STATE]
        xi = st[S5_STATE:]
        for j in range(nsteps):
            sh = 1 << j
            ar = ar_ref[j]
            ai = ai_ref[j]
            sr = shifted(xr, sh)
            si = shifted(xi, sh)
            xr, xi = xr + ar * sr - ai * si, xi + ar * si + ai * sr
        xprev = jnp.concatenate([shifted(xr, 1), shifted(xi, 1)], axis=0).astype(BF16)
        y = _dot(tt_ref[...], ut) + _dot(qt_ref[...], xprev)
        o_ref[b] = _gelu(y).astype(BF16).reshape(S5_BLOCK, S5_GROUP, r)


def _s5(zbt, tt, pt, qt, ar, ai):
    bsz, _, width, r = zbt.shape
    groups = width // S5_GROUP
    nsteps = ar.shape[1]
    u5 = zbt.reshape(bsz, S5_BLOCK, groups, S5_GROUP, r)
    rows = S5_BLOCK * S5_GROUP
    kern = functools.partial(_s5_kernel, nb=bsz, nsteps=nsteps)
    out = pl.pallas_call(
        kern,
        grid=(groups,),
        in_specs=[
            pl.BlockSpec((bsz, S5_BLOCK, None, S5_GROUP, r), lambda g: (0, 0, g, 0, 0)),
            pl.BlockSpec((None, rows, rows), lambda g: (g, 0, 0)),
            pl.BlockSpec((None, 2 * S5_STATE, rows), lambda g: (g, 0, 0)),
            pl.BlockSpec((None, rows, 2 * S5_STATE), lambda g: (g, 0, 0)),
            pl.BlockSpec((None, nsteps, S5_STATE, 1), lambda g: (g, 0, 0, 0)),
            pl.BlockSpec((None, nsteps, S5_STATE, 1), lambda g: (g, 0, 0, 0)),
        ],
        out_specs=pl.BlockSpec((bsz, S5_BLOCK, None, S5_GROUP, r), lambda g: (0, 0, g, 0, 0)),
        out_shape=jax.ShapeDtypeStruct(u5.shape, BF16),
        compiler_params=_params("parallel"),
        name="ev_s5",
    )(u5, tt, pt, qt, ar, ai)
    return out.reshape(zbt.shape)


def _s5_operators(lam_re, lam_im, log_dt, b_re, b_im, c_re, c_im, d, nsteps):
    groups, p = lam_re.shape
    c = S5_GROUP
    ln = S5_BLOCK
    lr = jnp.minimum(lam_re.astype(F32), DT_LAMBDA_RE_MAX)
    li = lam_im.astype(F32)
    dt = jnp.exp(log_dt.astype(F32))[:, None]
    zr = lr * dt
    zi = li * dt

    def powers(k):
        kk = jnp.asarray(k, F32)[None, :, None]
        mag = jnp.exp(zr[:, None, :] * kk)
        ang = zi[:, None, :] * kk
        return mag * jnp.cos(ang), mag * jnp.sin(ang)

    lbr, lbi = powers(jnp.ones((1,)))
    lbr = lbr[:, 0]
    lbi = lbi[:, 0]
    den = lr * lr + li * li
    fr = ((lbr - 1.0) * lr + lbi * li) / den
    fi = (lbi * lr - (lbr - 1.0) * li) / den
    bbr = fr[..., None] * b_re.astype(F32) - fi[..., None] * b_im.astype(F32)
    bbi = fr[..., None] * b_im.astype(F32) + fi[..., None] * b_re.astype(F32)
    cr = c_re.astype(F32)
    ci = c_im.astype(F32)

    pwr, pwi = powers(jnp.arange(ln + 1))
    mr = pwr[:, :ln, :, None] * bbr[:, None] - pwi[:, :ln, :, None] * bbi[:, None]
    mi = pwr[:, :ln, :, None] * bbi[:, None] + pwi[:, :ln, :, None] * bbr[:, None]
    kmat = jnp.einsum("gcp,gkpd->gkcd", cr, mr) - jnp.einsum("gcp,gkpd->gkcd", ci, mi)
    s_idx = jnp.arange(ln)[:, None]
    t_idx = jnp.arange(ln)[None, :]
    lag = t_idx - s_idx
    blocks = kmat[:, jnp.clip(lag, 0, ln - 1)]
    blocks = jnp.where((lag >= 0)[None, :, :, None, None], blocks, 0.0)
    skip = (jnp.eye(ln, dtype=F32)[None, :, :, None, None]
            * jnp.eye(c, dtype=F32)[None, None, None]
            * d.astype(F32).reshape(groups, 1, 1, c, 1))
    blocks = blocks + skip
    tt = blocks.transpose(0, 2, 3, 1, 4).reshape(groups, ln * c, ln * c)

    rev = ln - 1 - jnp.arange(ln)
    pr = pwr[:, rev][:, :, :, None] * bbr[:, None] - pwi[:, rev][:, :, :, None] * bbi[:, None]
    pi = pwr[:, rev][:, :, :, None] * bbi[:, None] + pwi[:, rev][:, :, :, None] * bbr[:, None]
    pt = jnp.concatenate([pr, pi], axis=2).transpose(0, 2, 1, 3).reshape(groups, 2 * p, ln * c)

    a_r = cr[:, None] * pwr[:, 1:ln + 1, None, :] - ci[:, None] * pwi[:, 1:ln + 1, None, :]
    a_i = cr[:, None] * pwi[:, 1:ln + 1, None, :] + ci[:, None] * pwr[:, 1:ln + 1, None, :]
    qt = jnp.concatenate([a_r, -a_i], axis=-1).reshape(groups, ln * c, 2 * p)

    sr, si = powers(ln * (2 ** jnp.arange(nsteps)))
    return (tt.astype(BF16), pt.astype(BF16), qt.astype(BF16), sr[..., None], si[..., None])


def _ev_out_kernel(x_ref, ya_ref, yb_ref, wglu_ref, bglu_ref, woa_ref, wob_ref, o_ref):
    yb = yb_ref[...]
    gate = _sigmoid(_dot(yb, wglu_ref[...]) + bglu_ref[...])
    y2 = (yb.astype(F32) * gate).astype(BF16)
    o_ref[...] = x_ref[...] + _dot(ya_ref[...], woa_ref[...]) + _dot(y2, wob_ref[...])


def _ev_out(x2, ya, yb, wglu, bglu, woa, wob, *, tm):
    n, d = x2.shape
    aw = ya.shape[1]
    bw = yb.shape[1]
    return pl.pallas_call(
        _ev_out_kernel,
        grid=(n // tm,),
        in_specs=[
            pl.BlockSpec((tm, d), lambda i: (i, 0)),
            pl.BlockSpec((tm, aw), lambda i: (i, 0)),
            pl.BlockSpec((tm, bw), lambda i: (i, 0)),
            pl.BlockSpec((bw, bw), lambda i: (0, 0)),
            pl.BlockSpec((1, bw), lambda i: (0, 0)),
            pl.BlockSpec((aw, d), lambda i: (0, 0)),
            pl.BlockSpec((bw, d), lambda i: (0, 0)),
        ],
        out_specs=pl.BlockSpec((tm, d), lambda i: (i, 0)),
        out_shape=jax.ShapeDtypeStruct((n, d), F32),
        compiler_params=_params("parallel"),
        name="ev_out",
    )(x2, ya, yb, wglu, bglu, woa, wob)


def _ffn_kernel(x_ref, gn_ref, wg_ref, wu_ref, wd_ref, o_ref, hn_ref, acc_ref):
    f = pl.program_id(1)

    @pl.when(f == 0)
    def _():
        hn_ref[...] = _rms(x_ref[...], gn_ref[...]).astype(BF16)
        acc_ref[...] = jnp.zeros_like(acc_ref)

    h = hn_ref[...]
    a = _dot(h, wg_ref[...])
    b = _dot(h, wu_ref[...])
    acc_ref[...] += _dot((a * _sigmoid(a) * b).astype(BF16), wd_ref[...])

    @pl.when(f == pl.num_programs(1) - 1)
    def _():
        o_ref[...] = x_ref[...] + acc_ref[...]


def _ffn(x2, gn, wg, wu, wd, *, tm, tf):
    n, d = x2.shape
    dff = wg.shape[1]
    return pl.pallas_call(
        _ffn_kernel,
        grid=(n // tm, dff // tf),
        in_specs=[
            pl.BlockSpec((tm, d), lambda i, f: (i, 0)),
            pl.BlockSpec((1, d), lambda i, f: (0, 0)),
            pl.BlockSpec((d, tf), lambda i, f: (0, f)),
            pl.BlockSpec((d, tf), lambda i, f: (0, f)),
            pl.BlockSpec((tf, d), lambda i, f: (f, 0)),
        ],
        out_specs=pl.BlockSpec((tm, d), lambda i, f: (i, 0)),
        out_shape=jax.ShapeDtypeStruct((n, d), F32),
        scratch_shapes=[pltpu.VMEM((tm, d), BF16), pltpu.VMEM((tm, d), F32)],
        compiler_params=_params("parallel", "arbitrary"),
        name="dense_ffn",
    )(x2, gn, wg, wu, wd)


def _route_kernel(x_ref, gn_ref, router_ref, tri_ref, hn_ref, meta_ref, cnt_ref, carry_ref):
    @pl.when(pl.program_id(0) == 0)
    def _():
        carry_ref[...] = jnp.zeros_like(carry_ref)

    hn = _rms(x_ref[...], gn_ref[...])
    hn_ref[...] = hn.astype(BF16)
    logits = jnp.dot(hn, router_ref[...], preferred_element_type=F32, precision=lax.Precision.HIGHEST)
    lane = lax.broadcasted_iota(I32, logits.shape, 1)
    neg = jnp.float32(-jnp.inf)
    logits = jnp.where(lane < N_EXPERTS, logits, neg)
    m1 = jnp.max(logits, axis=-1, keepdims=True)
    i1 = jnp.min(jnp.where(logits == m1, lane, ROUTER_LANES), axis=-1, keepdims=True)
    rest = jnp.where(lane == i1, neg, logits)
    m2 = jnp.max(rest, axis=-1, keepdims=True)
    i2 = jnp.min(jnp.where(rest == m2, lane, ROUTER_LANES), axis=-1, keepdims=True)
    e2 = jnp.exp(m2 - m1)
    w1 = 1.0 / (1.0 + e2)
    w2 = e2 / (1.0 + e2)
    chosen = jnp.where(lane == i1, 1.0, jnp.where(lane == i2, 1.0, 0.0))
    before = _dot(tri_ref[...], chosen.astype(BF16)) + carry_ref[...]
    r1 = jnp.sum(jnp.where(lane == i1, before, 0.0), axis=-1, keepdims=True)
    r2 = jnp.sum(jnp.where(lane == i2, before, 0.0), axis=-1, keepdims=True)
    cnt = jnp.sum(chosen, axis=0, keepdims=True)
    cnt_ref[...] = cnt
    carry_ref[...] += cnt
    fields = (i1.astype(F32), i2.astype(F32), r1, r2, w1, w2)
    meta = jnp.zeros_like(logits)
    for k, val in enumerate(fields):
        meta = jnp.where(lane == k, val, meta)
    meta_ref[...] = meta


def _route(x2, gn, router, *, tm):
    n, d = x2.shape
    nt = n // tm
    tri = jnp.tril(jnp.ones((tm, tm), F32), -1).astype(BF16)
    return pl.pallas_call(
        _route_kernel,
        grid=(nt,),
        in_specs=[
            pl.BlockSpec((tm, d), lambda i: (i, 0)),
            pl.BlockSpec((1, d), lambda i: (0, 0)),
            pl.BlockSpec((d, ROUTER_LANES), lambda i: (0, 0)),
            pl.BlockSpec((tm, tm), lambda i: (0, 0)),
        ],
        out_specs=[
            pl.BlockSpec((tm, d), lambda i: (i, 0)),
            pl.BlockSpec((tm, ROUTER_LANES), lambda i: (i, 0)),
            pl.BlockSpec((None, 1, ROUTER_LANES), lambda i: (i, 0, 0)),
        ],
        out_shape=[
            jax.ShapeDtypeStruct((n, d), BF16),
            jax.ShapeDtypeStruct((n, ROUTER_LANES), F32),
            jax.ShapeDtypeStruct((nt, 1, ROUTER_LANES), F32),
        ],
        scratch_shapes=[pltpu.VMEM((1, ROUTER_LANES), F32)],
        compiler_params=_params("arbitrary"),
        name="moe_route",
    )(x2, gn, router, tri)


def _moe_tables(counts, n_row_tiles):
    nt = counts.shape[0]
    t = MOE_TOKEN_TILE
    tf = MOE_ROW_TILE
    tot = counts.sum(0)
    padded = ((tot + tf - 1) // tf) * tf
    off = jnp.cumsum(padded) - padded
    before = jnp.cumsum(counts, axis=0) - counts
    start = off[None] + before
    last = start + jnp.maximum(counts, 1) - 1
    j0 = start // t
    j1 = last // t
    v0 = counts > 0
    v1 = v0 & (j1 != j0)
    ti = jnp.broadcast_to(jnp.arange(nt, dtype=I32)[:, None], counts.shape)
    cand_i = jnp.concatenate([ti.ravel(), ti.ravel()])
    cand_j = jnp.concatenate([j0.ravel(), j1.ravel()]).astype(I32)
    valid = jnp.concatenate([v0.ravel(), v1.ravel()])
    npairs = valid.sum().astype(I32)
    n_cand = cand_i.shape[0]
    clamp = jnp.minimum(jnp.arange(n_cand), npairs - 1)
    big = jnp.iinfo(jnp.int32).max

    def ordered(key):
        order = jnp.argsort(jnp.where(valid, key, big))[clamp]
        return cand_i[order], cand_j[order]

    n_disp_tiles = n_row_tiles * (tf // t)
    disp_i, disp_j = ordered(cand_j * nt + cand_i)
    comb_i, comb_j = ordered(cand_i * n_disp_tiles + cand_j)
    tile_start = jnp.arange(n_row_tiles, dtype=I32) * tf
    ends = (off + padded).astype(I32)
    tile_valid = (tile_start < ends[-1]).astype(I32)
    tile_expert = jnp.sum(tile_start[:, None] >= ends[None, :], axis=1).astype(I32)
    last_expert = jnp.max(jnp.where(tot > 0, jnp.arange(N_EXPERTS), 0)).astype(I32)
    tile_expert = jnp.where(tile_valid > 0, tile_expert, last_expert)
    return off.astype(I32), npairs.reshape(1), disp_i, disp_j, comb_i, comb_j, tile_expert, tile_valid


def _dispatch_kernel(np_ref, ti_ref, tj_ref, hn_ref, p1_ref, p2_ref, xs_in_ref, xs_ref):
    del xs_in_ref
    p = pl.program_id(0)
    j = tj_ref[p]
    first = (p == 0) | (tj_ref[jnp.maximum(p - 1, 0)] != j)

    @pl.when(p < np_ref[0])
    def _():
        t = xs_ref.shape[0]
        row = j * t + lax.broadcasted_iota(I32, (t, p1_ref.shape[-1]), 0)
        sel = jnp.where(row == p1_ref[...], 1.0, jnp.where(row == p2_ref[...], 1.0, 0.0))
        rows = _dot(sel.astype(BF16), hn_ref[...]).astype(BF16)

        @pl.when(first)
        def _():
            xs_ref[...] = rows

        @pl.when(jnp.logical_not(first))
        def _():
            xs_ref[...] += rows


def _dispatch(npairs, disp_i, disp_j, hn, pos1_rows, pos2_rows, n_rows):
    n, d = hn.shape
    t = MOE_TOKEN_TILE
    xs0 = jnp.zeros((n_rows, d), BF16)
    grid_spec = pltpu.PrefetchScalarGridSpec(
        num_scalar_prefetch=3,
        grid=(disp_i.shape[0],),
        in_specs=[
            pl.BlockSpec((t, d), lambda p, np_, ti, tj: (ti[p], 0)),
            pl.BlockSpec((None, 1, t), lambda p, np_, ti, tj: (ti[p], 0, 0)),
            pl.BlockSpec((None, 1, t), lambda p, np_, ti, tj: (ti[p], 0, 0)),
            pl.BlockSpec(memory_space=pl.ANY),
        ],
        out_specs=pl.BlockSpec((t, d), lambda p, np_, ti, tj: (tj[p], 0)),
    )
    return pl.pallas_call(
        _dispatch_kernel,
        grid_spec=grid_spec,
        out_shape=jax.ShapeDtypeStruct((n_rows, d), BF16),
        input_output_aliases={6: 0},
        compiler_params=_params("arbitrary"),
        name="moe_dispatch",
    )(npairs, disp_i, disp_j, hn, pos1_rows, pos2_rows, xs0)


def _experts_kernel(te_ref, tv_ref, xs_ref, wg_ref, wu_ref, wd_ref, ys_ref, acc_ref):
    del te_ref
    j = pl.program_id(0)
    f = pl.program_id(1)
    valid = tv_ref[j] > 0

    @pl.when(f == 0)
    def _():
        acc_ref[...] = jnp.zeros_like(acc_ref)

    @pl.when(valid)
    def _():
        h = xs_ref[...]
        a = _dot(h, wg_ref[...])
        b = _dot(h, wu_ref[...])
        acc_ref[...] += _dot((a * _sigmoid(a) * b).astype(BF16), wd_ref[...])

    @pl.when(f == pl.num_programs(1) - 1)
    def _():
        ys_ref[...] = acc_ref[...].astype(BF16)


def _experts(tile_expert, tile_valid, xs, wg, wu, wd, *, tf):
    n_rows, d = xs.shape
    dff = wg.shape[2]
    tm = MOE_ROW_TILE
    grid_spec = pltpu.PrefetchScalarGridSpec(
        num_scalar_prefetch=2,
        grid=(n_rows // tm, dff // tf),
        in_specs=[
            pl.BlockSpec((tm, d), lambda j, f, te, tv: (j, 0)),
            pl.BlockSpec((None, d, tf), lambda j, f, te, tv: (te[j], 0, f * tv[j])),
            pl.BlockSpec((None, d, tf), lambda j, f, te, tv: (te[j], 0, f * tv[j])),
            pl.BlockSpec((None, tf, d), lambda j, f, te, tv: (te[j], f * tv[j], 0)),
        ],
        out_specs=pl.BlockSpec((tm, d), lambda j, f, te, tv: (j, 0)),
        scratch_shapes=[pltpu.VMEM((tm, d), F32)],
    )
    return pl.pallas_call(
        _experts_kernel,
        grid_spec=grid_spec,
        out_shape=jax.ShapeDtypeStruct((n_rows, d), BF16),
        compiler_params=_params("parallel", "arbitrary"),
        name="moe_experts",
    )(tile_expert, tile_valid, xs, wg, wu, wd)


def _combine_kernel(np_ref, ti_ref, tj_ref, x_ref, ys_ref, p1_ref, p2_ref, w1_ref, w2_ref, o_ref):
    p = pl.program_id(0)
    i = ti_ref[p]
    first = (p == 0) | (ti_ref[jnp.maximum(p - 1, 0)] != i)

    @pl.when(p < np_ref[0])
    def _():
        t = ys_ref.shape[0]
        col = tj_ref[p] * t + lax.broadcasted_iota(I32, (x_ref.shape[0], t), 1)
        sel = jnp.where(col == p1_ref[...], w1_ref[...], jnp.where(col == p2_ref[...], w2_ref[...], 0.0))
        upd = _dot(sel.astype(BF16), ys_ref[...])

        @pl.when(first)
        def _():
            o_ref[...] = x_ref[...] + upd

        @pl.when(jnp.logical_not(first))
        def _():
            o_ref[...] += upd


def _combine(npairs, comb_i, comb_j, x2, ys, pos1, pos2, w1, w2):
    n, d = x2.shape
    t = MOE_TOKEN_TILE
    tok = lambda p, np_, ti, tj: (ti[p], 0)
    grid_spec = pltpu.PrefetchScalarGridSpec(
        num_scalar_prefetch=3,
        grid=(comb_i.shape[0],),
        in_specs=[
            pl.BlockSpec((t, d), tok),
            pl.BlockSpec((t, d), lambda p, np_, ti, tj: (tj[p], 0)),
            pl.BlockSpec((t, 1), tok),
            pl.BlockSpec((t, 1), tok),
            pl.BlockSpec((t, 1), tok),
            pl.BlockSpec((t, 1), tok),
        ],
        out_specs=pl.BlockSpec((t, d), tok),
    )
    return pl.pallas_call(
        _combine_kernel,
        grid_spec=grid_spec,
        out_shape=jax.ShapeDtypeStruct((n, d), F32),
        compiler_params=_params("arbitrary"),
        name="moe_combine",
    )(npairs, comb_i, comb_j, x2, ys, pos1, pos2, w1, w2)


def _moe(x2, gn, router, wg, wu, wd):
    n, d = x2.shape
    t = MOE_TOKEN_TILE
    nt = n // t
    n_row_tiles = (2 * n + N_EXPERTS * (MOE_ROW_TILE - 1)) // MOE_ROW_TILE
    hn, meta, cnt = _route(x2, gn, router, tm=t)
    counts = cnt[:, 0, :N_EXPERTS].astype(I32)
    off, npairs, disp_i, disp_j, comb_i, comb_j, tile_expert, tile_valid = _moe_tables(counts, n_row_tiles)
    i1 = meta[:, 0].astype(I32)
    i2 = meta[:, 1].astype(I32)
    pos1 = off[i1] + meta[:, 2].astype(I32)
    pos2 = off[i2] + meta[:, 3].astype(I32)
    xs = _dispatch(npairs, disp_i, disp_j, hn, pos1.reshape(nt, 1, t), pos2.reshape(nt, 1, t),
                   n_row_tiles * MOE_ROW_TILE)
    ys = _experts(tile_expert, tile_valid, xs, wg, wu, wd, tf=wg.shape[2] // 4)
    return _combine(npairs, comb_i, comb_j, x2, ys, pos1.reshape(n, 1), pos2.reshape(n, 1),
                    meta[:, 4:5], meta[:, 5:6])


def _qkv_kernel(x_ref, gn_ref, wqk_ref, wvt_ref, seg_ref, gqk_ref, qk_ref, vt_ref):
    h = _rms(x_ref[...], gn_ref[...]).astype(BF16)
    vt_ref[...] = _dot_nt(wvt_ref[...], h).astype(BF16)
    z = _dot(h, wqk_ref[...])
    seg = seg_ref[...]
    sw = seg.shape[0]
    for c in range(z.shape[1] // sw):
        cols = slice(c * sw, (c + 1) * sw)
        zc = z[:, cols]
        ms = _dot((zc * zc).astype(BF16), seg) * (1.0 / ATT_HEAD_DIM)
        qk_ref[:, cols] = (zc * lax.rsqrt(ms + EPS) * gqk_ref[:, cols]).astype(BF16)


def _qkv(x2, gn, wqk, wvt, seg, gqk, *, bsz, tm):
    n, d = x2.shape
    seq = n // bsz
    qk_cols = wqk.shape[1]
    v_cols = wvt.shape[0]
    per_b = seq // tm
    return pl.pallas_call(
        _qkv_kernel,
        grid=(bsz, per_b),
        in_specs=[
            pl.BlockSpec((tm, d), lambda b, i: (b * per_b + i, 0)),
            pl.BlockSpec((1, d), lambda b, i: (0, 0)),
            pl.BlockSpec((d, qk_cols), lambda b, i: (0, 0)),
            pl.BlockSpec((v_cols, d), lambda b, i: (0, 0)),
            pl.BlockSpec(seg.shape, lambda b, i: (0, 0)),
            pl.BlockSpec((1, qk_cols), lambda b, i: (0, 0)),
        ],
        out_specs=[
            pl.BlockSpec((tm, qk_cols), lambda b, i: (b * per_b + i, 0)),
            pl.BlockSpec((None, v_cols, tm), lambda b, i: (b, 0, i)),
        ],
        out_shape=[
            jax.ShapeDtypeStruct((n, qk_cols), BF16),
            jax.ShapeDtypeStruct((bsz, v_cols, seq), BF16),
        ],
        compiler_params=_params("parallel", "parallel"),
        name="od_qkv",
    )(x2, gn, wqk, wvt, seg, gqk)


def _attn_kernel(qi_ref, ki_ref, scal_ref, q_ref, k_ref, vt_ref, sg_ref, o_ref, qm_ref, l_ref, acc_ref, *maybe_m_ref,
                 bq, bk, out_scale, running_max):
    p = pl.program_id(2)
    qi = qi_ref[p]
    ki = ki_ref[p]
    last_k = ((qi + 1) * bq - 1) // bk

    @pl.when(ki == 0)
    def _():
        q = q_ref[...]
        lane = lax.broadcasted_iota(I32, q.shape, 1)
        zero = jnp.zeros_like(q)
        qm_ref[0] = jnp.where(lane < ATT_HEAD_DIM, q, zero)
        qm_ref[1] = jnp.where(lane >= ATT_HEAD_DIM, q, zero)
        l_ref[...] = jnp.zeros_like(l_ref)
        acc_ref[...] = jnp.zeros_like(acc_ref)
        if running_max:
            maybe_m_ref[0][...] = jnp.full_like(maybe_m_ref[0], -jnp.inf)

    def step(masked):
        k = k_ref[...]
        vt = vt_ref[...]
        if masked:
            kpos = ki * bk + lax.broadcasted_iota(I32, (bk, bq), 0)
            qpos = qi * bq + lax.broadcasted_iota(I32, (bk, bq), 1)
            keep = kpos <= qpos
        for m in range(2):
            s = _dot_nt(k, qm_ref[m])
            if masked:
                s = jnp.where(keep, s, -jnp.inf)
            if running_max:
                m_ref = maybe_m_ref[0]
                m_old = m_ref[m]
                m_new = jnp.maximum(m_old, jnp.max(s, axis=0, keepdims=True))
                pt = jnp.exp2(s - m_new)
                alpha = jnp.exp2(m_old - m_new)
                l_ref[m] = alpha * l_ref[m] + jnp.sum(pt, axis=0, keepdims=True)
                acc_ref[m] = alpha * acc_ref[m] + _dot(vt, pt.astype(BF16))
                m_ref[m] = m_new
            else:
                pt = jnp.exp2(s)
                l_ref[m] += jnp.sum(pt, axis=0, keepdims=True)
                acc_ref[m] += _dot(vt, pt.astype(BF16))

    def step_plain(masked):
        s = _dot_nt(k_ref[...], qm_ref[...].reshape(2 * bq, qm_ref.shape[-1]))
        if masked:
            kpos = ki * bk + lax.broadcasted_iota(I32, s.shape, 0)
            qcol = lax.broadcasted_iota(I32, s.shape, 1)
            qpos = qi * bq + jnp.where(qcol >= bq, qcol - bq, qcol)
            s = jnp.where(kpos <= qpos, s, -jnp.inf)
        pt = jnp.exp2(s)
        lsum = jnp.sum(pt, axis=0, keepdims=True)
        pv = _dot(vt_ref[...], pt.astype(BF16))
        for m in range(2):
            l_ref[m] += lsum[:, m * bq:(m + 1) * bq]
            acc_ref[m] += pv[:, m * bq:(m + 1) * bq]

    if not running_max:
        step = step_plain

    fully_visible = (ki + 1) * bk - 1 <= qi * bq

    @pl.when(fully_visible)
    def _():
        step(False)

    @pl.when(jnp.logical_not(fully_visible))
    def _():
        step(True)

    @pl.when(ki == last_k)
    def _():
        o = acc_ref[0] / l_ref[0] - scal_ref[0] * (acc_ref[1] / l_ref[1])
        ms = jnp.mean(o * o, axis=0, keepdims=True)
        o = o * lax.rsqrt(ms + EPS) * (sg_ref[...] * out_scale)
        o_ref[...] = o.T.astype(BF16)


def _attention(qk3, vt3, scal, subln_col, *, bq, bk, lambda_init, running_max):
    bsz, seq, _ = qk3.shape
    dv = 2 * ATT_HEAD_DIM
    nh = ATT_HEADS
    pairs = [(qi, ki) for qi in range(seq // bq) for ki in range(((qi + 1) * bq - 1) // bk + 1)]
    qi_tab = jnp.asarray(np.array([p[0] for p in pairs], np.int32))
    ki_tab = jnp.asarray(np.array([p[1] for p in pairs], np.int32))
    kern = functools.partial(_attn_kernel, bq=bq, bk=bk, out_scale=1.0 - lambda_init, running_max=running_max)
    scratch = [
        pltpu.VMEM((2, bq, dv), BF16),
        pltpu.VMEM((2, 1, bq), F32),
        pltpu.VMEM((2, dv, bq), F32),
    ]
    if running_max:
        scratch.append(pltpu.VMEM((2, 1, bq), F32))
    grid_spec = pltpu.PrefetchScalarGridSpec(
        num_scalar_prefetch=2,
        grid=(bsz, nh, len(pairs)),
        in_specs=[
            pl.BlockSpec(memory_space=pltpu.SMEM),
            pl.BlockSpec((None, bq, dv), lambda b, h, p, qt, kt: (b, qt[p], h)),
            pl.BlockSpec((None, bk, dv), lambda b, h, p, qt, kt: (b, kt[p], nh + h)),
            pl.BlockSpec((None, dv, bk), lambda b, h, p, qt, kt: (b, h, kt[p])),
            pl.BlockSpec((dv, 1), lambda b, h, p, qt, kt: (0, 0)),
        ],
        out_specs=pl.BlockSpec((None, bq, dv), lambda b, h, p, qt, kt: (b, qt[p], h)),
        scratch_shapes=scratch,
    )
    return pl.pallas_call(
        kern,
        grid_spec=grid_spec,
        out_shape=jax.ShapeDtypeStruct((bsz, seq, nh * dv), BF16),
        compiler_params=_params("parallel", "parallel", "arbitrary"),
        name="od_attn_online" if running_max else "od_attn",
    )(qi_tab, ki_tab, scal, qk3, qk3, vt3, subln_col)


def _proj_res_kernel(x_ref, y_ref, w_ref, o_ref):
    o_ref[...] = x_ref[...] + _dot(y_ref[...], w_ref[...])


def _proj_res(x2, y, w, *, tm):
    n, d = x2.shape
    k = y.shape[1]
    return pl.pallas_call(
        _proj_res_kernel,
        grid=(n // tm,),
        in_specs=[
            pl.BlockSpec((tm, d), lambda i: (i, 0)),
            pl.BlockSpec((tm, k), lambda i: (i, 0)),
            pl.BlockSpec((k, d), lambda i: (0, 0)),
        ],
        out_specs=pl.BlockSpec((tm, d), lambda i: (i, 0)),
        out_shape=jax.ShapeDtypeStruct((n, d), F32),
        compiler_params=_params("parallel"),
        name="od_out",
    )(x2, y, w)


def _row(v):
    return v.astype(F32).reshape(1, -1)


def _pick(n, pref):
    t = min(n, pref)
    assert n % t == 0, (n, pref)
    return t


def kernel(x, ev_norm_mix, ev_w_in, ev_gmlp_ln_g, ev_gmlp_ln_b, ev_gmlp_ws, ev_gmlp_bs, ev_s5_lambda_re, ev_s5_lambda_im, ev_s5_log_dt, ev_s5_b_re, ev_s5_b_im, ev_s5_c_re, ev_s5_c_im, ev_s5_d, ev_s5_w_glu, ev_s5_b_glu, ev_w_out, ev_norm_ffn, ev_ffn_w_gate, ev_ffn_w_up, ev_ffn_w_down, od_norm_mix, od_w_in, od_q_norm, od_k_norm, od_lambda_q1, od_lambda_k1, od_lambda_q2, od_lambda_k2, od_subln, od_w_out, od_norm_ffn, od_router, od_moe_w_gate, od_moe_w_up, od_moe_w_down):
    bsz, seq, d = x.shape
    n = bsz * seq
    depth = ev_norm_mix.shape[0] + od_norm_mix.shape[0]
    a_width = ev_gmlp_ln_g.shape[-1]
    assert seq % (S5_BLOCK * 2) == 0 and seq % GMLP_CHUNK == 0 and n % MOE_TOKEN_TILE == 0
    r = seq // S5_BLOCK
    nsteps = max(1, (r - 1).bit_length())

    tm_in = _pick(seq, 512)
    bq = _pick(seq, 1024)
    bk = _pick(seq, 1024)
    qk_cols = 2 * ATT_HEADS * 2 * ATT_HEAD_DIM
    tril = jnp.tril(jnp.ones((GMLP_CHUNK, GMLP_CHUNK), F32))
    seg_id = jnp.arange(4 * ATT_HEAD_DIM) // ATT_HEAD_DIM
    seg = (seg_id[:, None] == seg_id[None, :]).astype(BF16)

    x2 = x.reshape(n, d)
    for layer in range(depth):
        i = layer // 2
        if layer % 2 == 0:
            wm = (ev_gmlp_ws[i].astype(F32) * tril).astype(BF16)
            bias = jnp.repeat(ev_gmlp_bs[i].astype(F32).T, a_width // A_HEADS, axis=1)
            ya, zb = _ev_in(x2, _row(ev_norm_mix[i]), ev_w_in[i].astype(BF16), _row(ev_gmlp_ln_g[i]),
                            _row(ev_gmlp_ln_b[i]), wm, bias, tm=tm_in)
            bw = zb.shape[1]
            zbt = zb.reshape(bsz, r, S5_BLOCK, bw).transpose(0, 2, 3, 1)
            ops = _s5_operators(ev_s5_lambda_re[i], ev_s5_lambda_im[i], ev_s5_log_dt[i], ev_s5_b_re[i],
                                ev_s5_b_im[i], ev_s5_c_re[i], ev_s5_c_im[i], ev_s5_d[i], nsteps)
            ybt = _s5(zbt, *ops)
            yb = ybt.transpose(0, 3, 1, 2).reshape(n, bw)
            w_out = ev_w_out[i].astype(BF16)
            x2 = _ev_out(x2, ya, yb, ev_s5_w_glu[i].astype(BF16), _row(ev_s5_b_glu[i]),
                         w_out[:a_width], w_out[a_width:], tm=tm_in)
            x2 = _ffn(x2, _row(ev_norm_ffn[i]), ev_ffn_w_gate[i].astype(BF16), ev_ffn_w_up[i].astype(BF16),
                      ev_ffn_w_down[i].astype(BF16), tm=tm_in, tf=ev_ffn_w_gate.shape[-1] // 2)
        else:
            lambda_init = 0.8 - 0.6 * math.exp(-0.3 * layer)
            gqk = jnp.concatenate([
                jnp.tile(od_q_norm[i].astype(F32) * (ATT_HEAD_DIM ** -0.5 * LOG2E), 2 * ATT_HEADS),
                jnp.tile(od_k_norm[i].astype(F32), 2 * ATT_HEADS)]).reshape(1, -1)
            w_in = od_w_in[i].astype(BF16)
            qk, vt = _qkv(x2, _row(od_norm_mix[i]), w_in[:, :qk_cols], w_in[:, qk_cols:].T, seg, gqk,
                          bsz=bsz, tm=tm_in)
            lam = (jnp.exp(jnp.sum(od_lambda_q1[i].astype(F32) * od_lambda_k1[i].astype(F32)))
                   - jnp.exp(jnp.sum(od_lambda_q2[i].astype(F32) * od_lambda_k2[i].astype(F32)))
                   + lambda_init)
            bound = (ATT_HEAD_DIM * LOG2E * ATT_HEAD_DIM ** -0.5
                     * jnp.max(jnp.abs(od_q_norm[i].astype(F32))) * jnp.max(jnp.abs(od_k_norm[i].astype(F32))))
            scal = jnp.stack([lam, bound])
            attend = functools.partial(_attention, bq=bq, bk=bk, lambda_init=lambda_init)
            yc = lax.cond(bound <= ATT_STATIC_MAX_LIMIT,
                          functools.partial(attend, running_max=False),
                          functools.partial(attend, running_max=True),
                          qk.reshape(bsz, seq, qk_cols), vt, scal, od_subln[i].astype(F32).reshape(-1, 1))
            x2 = _proj_res(x2, yc.reshape(n, -1), od_w_out[i].astype(BF16), tm=tm_in)
            router = jnp.pad(od_router[i].astype(F32), ((0, 0), (0, ROUTER_LANES - N_EXPERTS)))
            x2 = _moe(x2, _row(od_norm_ffn[i]), router, od_moe_w_gate[i].astype(BF16),
                      od_moe_w_up[i].astype(BF16), od_moe_w_down[i].astype(BF16))
    return x2.reshape(bsz, seq, d)
```

```python
import functools
import math

import numpy as np
import jax
import jax.numpy as jnp
from jax import lax
from jax.experimental import pallas as pl
from jax.experimental.pallas import tpu as pltpu

F32 = jnp.float32
BF16 = jnp.bfloat16
I32 = jnp.int32

EPS = 1e-6
A_HEADS = 4
GMLP_CHUNK = 128
S5_GROUP = 16
S5_STATE = 64
S5_BLOCK = 16
DT_LAMBDA_RE_MAX = -1e-4
ATT_HEADS = 8
ATT_HEAD_DIM = 64
N_EXPERTS = 8
ROUTER_LANES = 128
MOE_TOKEN_TILE = 512
MOE_ROW_TILE = 1024
MOE_GROUP_ALIGN = 16
MOE_PIECES = -(-(2 * MOE_TOKEN_TILE + N_EXPERTS * (MOE_GROUP_ALIGN - 1)) // MOE_GROUP_ALIGN)
LOG2E = 1.4426950408889634
ATT_STATIC_MAX_LIMIT = 40.0
VMEM_LIMIT = 56 * 1024 * 1024


def _params(*sem):
    return pltpu.CompilerParams(dimension_semantics=sem, vmem_limit_bytes=VMEM_LIMIT)


def _rms(x, g):
    return x * lax.rsqrt(jnp.mean(x * x, axis=-1, keepdims=True) + EPS) * g


def _gelu(x):
    return 0.5 * x * (1.0 + jnp.tanh(0.7978845608028654 * (x + 0.044715 * (x * x * x))))


def _sigmoid(x):
    return 1.0 / (1.0 + jnp.exp(-x))


def _dot(a, b):
    return jnp.dot(a, b, preferred_element_type=F32)


def _dot_nt(a, b):
    return lax.dot_general(a, b, (((1,), (1,)), ((), ())), preferred_element_type=F32)


def _ev_in_kernel(x_ref, gn_ref, w_ref, lng_ref, lnb_ref, wm_ref, bias_ref, ya_ref, zb_ref, *, n_chunk, a_width):
    h = _rms(x_ref[...], gn_ref[...]).astype(BF16)
    z = _dot(h, w_ref[...])
    zb_ref[...] = z[:, 2 * a_width:].astype(BF16)
    g = _gelu(z[:, :2 * a_width])
    u = g[:, :a_width]
    v = g[:, a_width:]
    mu = jnp.mean(v, axis=-1, keepdims=True)
    vc = v - mu
    vn = vc * lax.rsqrt(jnp.mean(vc * vc, axis=-1, keepdims=True) + EPS) * lng_ref[...] + lnb_ref[...]
    vb = vn.astype(BF16)
    hd = a_width // A_HEADS
    for c in range(n_chunk):
        rows = slice(c * GMLP_CHUNK, (c + 1) * GMLP_CHUNK)
        for k in range(A_HEADS):
            cols = slice(k * hd, (k + 1) * hd)
            s = _dot(wm_ref[k], vb[rows, cols]) + bias_ref[:, cols]
            ya_ref[rows, cols] = (u[rows, cols] * s).astype(BF16)


def _ev_in(x2, gn, w_in, ln_g, ln_b, wm, bias, *, tm):
    n, d = x2.shape
    cols = w_in.shape[1]
    a_width = ln_g.shape[-1]
    b_width = cols - 2 * a_width
    kern = functools.partial(_ev_in_kernel, n_chunk=tm // GMLP_CHUNK, a_width=a_width)
    return pl.pallas_call(
        kern,
        grid=(n // tm,),
        in_specs=[
            pl.BlockSpec((tm, d), lambda i: (i, 0)),
            pl.BlockSpec((1, d), lambda i: (0, 0)),
            pl.BlockSpec((d, cols), lambda i: (0, 0)),
            pl.BlockSpec((1, a_width), lambda i: (0, 0)),
            pl.BlockSpec((1, a_width), lambda i: (0, 0)),
            pl.BlockSpec((A_HEADS, GMLP_CHUNK, GMLP_CHUNK), lambda i: (0, 0, 0)),
            pl.BlockSpec((GMLP_CHUNK, a_width), lambda i: (0, 0)),
        ],
        out_specs=[
            pl.BlockSpec((tm, a_width), lambda i: (i, 0)),
            pl.BlockSpec((tm, b_width), lambda i: (i, 0)),
        ],
        out_shape=[
            jax.ShapeDtypeStruct((n, a_width), BF16),
            jax.ShapeDtypeStruct((n, b_width), BF16),
        ],
        compiler_params=_params("parallel"),
        name="ev_in_gmlp",
    )(x2, gn, w_in, ln_g, ln_b, wm, bias)


def _s5_kernel(u_ref, tt_ref, pt_ref, qt_ref, ar_ref, ai_ref, o_ref, *, nb, nsteps):
    r = u_ref.shape[-1]
    rows = S5_BLOCK * S5_GROUP
    lane = lax.broadcasted_iota(jnp.int32, (S5_STATE, r), 1)

    def shifted(a, sh):
        return jnp.where(lane >= sh, pltpu.roll(a, sh, 1), 0.0)

    for b in range(nb):
        ut = u_ref[b].reshape(rows, r)
        st = _dot(pt_ref[...], ut)
        xr = st[:S5_STATE]
        xi = st[S5_STATE:]
        for j in range(nsteps):
            sh = 1 << j
            ar = ar_ref[j]
            ai = ai_ref[j]
            sr = shifted(xr, sh)
            si = shifted(xi, sh)
            xr, xi = xr + ar * sr - ai * si, xi + ar * si + ai * sr
        xprev = jnp.concatenate([shifted(xr, 1), shifted(xi, 1)], axis=0).astype(BF16)
        y = _dot(tt_ref[...], ut) + _dot(qt_ref[...], xprev)
        o_ref[b] = _gelu(y).astype(BF16).reshape(S5_BLOCK, S5_GROUP, r)


def _s5(zbt, tt, pt, qt, ar, ai):
    bsz, _, width, r = zbt.shape
    groups = width // S5_GROUP
    nsteps = ar.shape[1]
    u5 = zbt.reshape(bsz, S5_BLOCK, groups, S5_GROUP, r)
    rows = S5_BLOCK * S5_GROUP
    kern = functools.partial(_s5_kernel, nb=bsz, nsteps=nsteps)
    out = pl.pallas_call(
        kern,
        grid=(groups,),
        in_specs=[
            pl.BlockSpec((bsz, S5_BLOCK, None, S5_GROUP, r), lambda g: (0, 0, g, 0, 0)),
            pl.BlockSpec((None, rows, rows), lambda g: (g, 0, 0)),
            pl.BlockSpec((None, 2 * S5_STATE, rows), lambda g: (g, 0, 0)),
            pl.BlockSpec((None, rows, 2 * S5_STATE), lambda g: (g, 0, 0)),
            pl.BlockSpec((None, nsteps, S5_STATE, 1), lambda g: (g, 0, 0, 0)),
            pl.BlockSpec((None, nsteps, S5_STATE, 1), lambda g: (g, 0, 0, 0)),
        ],
        out_specs=pl.BlockSpec((bsz, S5_BLOCK, None, S5_GROUP, r), lambda g: (0, 0, g, 0, 0)),
        out_shape=jax.ShapeDtypeStruct(u5.shape, BF16),
        compiler_params=_params("parallel"),
        name="ev_s5",
    )(u5, tt, pt, qt, ar, ai)
    return out.reshape(zbt.shape)


def _s5_operators(lam_re, lam_im, log_dt, b_re, b_im, c_re, c_im, d, nsteps):
    groups, p = lam_re.shape
    c = S5_GROUP
    ln = S5_BLOCK
    lr = jnp.minimum(lam_re.astype(F32), DT_LAMBDA_RE_MAX)
    li = lam_im.astype(F32)
    dt = jnp.exp(log_dt.astype(F32))[:, None]
    zr = lr * dt
    zi = li * dt

    def powers(k):
        kk = jnp.asarray(k, F32)[None, :, None]
        mag = jnp.exp(zr[:, None, :] * kk)
        ang = zi[:, None, :] * kk
        return mag * jnp.cos(ang), mag * jnp.sin(ang)

    lbr, lbi = powers(jnp.ones((1,)))
    lbr = lbr[:, 0]
    lbi = lbi[:, 0]
    den = lr * lr + li * li
    fr = ((lbr - 1.0) * lr + lbi * li) / den
    fi = (lbi * lr - (lbr - 1.0) * li) / den
    bbr = fr[..., None] * b_re.astype(F32) - fi[..., None] * b_im.astype(F32)
    bbi = fr[..., None] * b_im.astype(F32) + fi[..., None] * b_re.astype(F32)
    cr = c_re.astype(F32)
    ci = c_im.astype(F32)

    pwr, pwi = powers(jnp.arange(ln + 1))
    mr = pwr[:, :ln, :, None] * bbr[:, None] - pwi[:, :ln, :, None] * bbi[:, None]
    mi = pwr[:, :ln, :, None] * bbi[:, None] + pwi[:, :ln, :, None] * bbr[:, None]
    kmat = jnp.einsum("gcp,gkpd->gkcd", cr, mr) - jnp.einsum("gcp,gkpd->gkcd", ci, mi)
    s_idx = jnp.arange(ln)[:, None]
    t_idx = jnp.arange(ln)[None, :]
    lag = t_idx - s_idx
    blocks = kmat[:, jnp.clip(lag, 0, ln - 1)]
    blocks = jnp.where((lag >= 0)[None, :, :, None, None], blocks, 0.0)
    skip = (jnp.eye(ln, dtype=F32)[None, :, :, None, None]
            * jnp.eye(c, dtype=F32)[None, None, None]
            * d.astype(F32).reshape(groups, 1, 1, c, 1))
    blocks = blocks + skip
    tt = blocks.transpose(0, 2, 3, 1, 4).reshape(groups, ln * c, ln * c)

    rev = ln - 1 - jnp.arange(ln)
    pr = pwr[:, rev][:, :, :, None] * bbr[:, None] - pwi[:, rev][:, :, :, None] * bbi[:, None]
    pi = pwr[:, rev][:, :, :, None] * bbi[:, None] + pwi[:, rev][:, :, :, None] * bbr[:, None]
    pt = jnp.concatenate([pr, pi], axis=2).transpose(0, 2, 1, 3).reshape(groups, 2 * p, ln * c)

    a_r = cr[:, None] * pwr[:, 1:ln + 1, None, :] - ci[:, None] * pwi[:, 1:ln + 1, None, :]
    a_i = cr[:, None] * pwi[:, 1:ln + 1, None, :] + ci[:, None] * pwr[:, 1:ln + 1, None, :]
    qt = jnp.concatenate([a_r, -a_i], axis=-1).reshape(groups, ln * c, 2 * p)

    sr, si = powers(ln * (2 ** jnp.arange(nsteps)))
    return (tt.astype(BF16), pt.astype(BF16), qt.astype(BF16), sr[..., None], si[..., None])


def _ev_out_kernel(x_ref, ya_ref, yb_ref, wglu_ref, bglu_ref, woa_ref, wob_ref, o_ref):
    yb = yb_ref[...]
    gate = _sigmoid(_dot(yb, wglu_ref[...]) + bglu_ref[...])
    y2 = (yb.astype(F32) * gate).astype(BF16)
    o_ref[...] = x_ref[...] + _dot(ya_ref[...], woa_ref[...]) + _dot(y2, wob_ref[...])


def _ev_out(x2, ya, yb, wglu, bglu, woa, wob, *, tm):
    n, d = x2.shape
    aw = ya.shape[1]
    bw = yb.shape[1]
    return pl.pallas_call(
        _ev_out_kernel,
        grid=(n // tm,),
        in_specs=[
            pl.BlockSpec((tm, d), lambda i: (i, 0)),
            pl.BlockSpec((tm, aw), lambda i: (i, 0)),
            pl.BlockSpec((tm, bw), lambda i: (i, 0)),
            pl.BlockSpec((bw, bw), lambda i: (0, 0)),
            pl.BlockSpec((1, bw), lambda i: (0, 0)),
            pl.BlockSpec((aw, d), lambda i: (0, 0)),
            pl.BlockSpec((bw, d), lambda i: (0, 0)),
        ],
        out_specs=pl.BlockSpec((tm, d), lambda i: (i, 0)),
        out_shape=jax.ShapeDtypeStruct((n, d), F32),
        compiler_params=_params("parallel"),
        name="ev_out",
    )(x2, ya, yb, wglu, bglu, woa, wob)


def _ffn_kernel(x_ref, gn_ref, wg_ref, wu_ref, wd_ref, o_ref, hn_ref, acc_ref):
    f = pl.program_id(1)

    @pl.when(f == 0)
    def _():
        hn_ref[...] = _rms(x_ref[...], gn_ref[...]).astype(BF16)
        acc_ref[...] = jnp.zeros_like(acc_ref)

    h = hn_ref[...]
    a = _dot(h, wg_ref[...])
    b = _dot(h, wu_ref[...])
    acc_ref[...] += _dot((a * _sigmoid(a) * b).astype(BF16), wd_ref[...])

    @pl.when(f == pl.num_programs(1) - 1)
    def _():
        o_ref[...] = x_ref[...] + acc_ref[...]


def _ffn(x2, gn, wg, wu, wd, *, tm, tf):
    n, d = x2.shape
    dff = wg.shape[1]
    return pl.pallas_call(
        _ffn_kernel,
        grid=(n // tm, dff // tf),
        in_specs=[
            pl.BlockSpec((tm, d), lambda i, f: (i, 0)),
            pl.BlockSpec((1, d), lambda i, f: (0, 0)),
            pl.BlockSpec((d, tf), lambda i, f: (0, f)),
            pl.BlockSpec((d, tf), lambda i, f: (0, f)),
            pl.BlockSpec((tf, d), lambda i, f: (f, 0)),
        ],
        out_specs=pl.BlockSpec((tm, d), lambda i, f: (i, 0)),
        out_shape=jax.ShapeDtypeStruct((n, d), F32),
        scratch_shapes=[pltpu.VMEM((tm, d), BF16), pltpu.VMEM((tm, d), F32)],
        compiler_params=_params("parallel", "arbitrary"),
        name="dense_ffn",
    )(x2, gn, wg, wu, wd)


def _route_kernel(x_ref, gn_ref, router_ref, tri_ref, hn_ref, meta_ref, cnt_ref):
    hn = _rms(x_ref[...], gn_ref[...])
    hn_ref[...] = hn.astype(BF16)
    logits = jnp.dot(hn, router_ref[...], preferred_element_type=F32, precision=lax.Precision.HIGHEST)
    lane = lax.broadcasted_iota(I32, logits.shape, 1)
    neg = jnp.float32(-jnp.inf)
    logits = jnp.where(lane < N_EXPERTS, logits, neg)
    m1 = jnp.max(logits, axis=-1, keepdims=True)
    i1 = jnp.min(jnp.where(logits == m1, lane, ROUTER_LANES), axis=-1, keepdims=True)
    rest = jnp.where(lane == i1, neg, logits)
    m2 = jnp.max(rest, axis=-1, keepdims=True)
    i2 = jnp.min(jnp.where(rest == m2, lane, ROUTER_LANES), axis=-1, keepdims=True)
    e2 = jnp.exp(m2 - m1)
    w1 = 1.0 / (1.0 + e2)
    w2 = e2 / (1.0 + e2)
    chosen = jnp.where(lane == i1, 1.0, jnp.where(lane == i2, 1.0, 0.0))
    before = _dot(tri_ref[...], chosen.astype(BF16))
    r1 = jnp.sum(jnp.where(lane == i1, before, 0.0), axis=-1, keepdims=True)
    r2 = jnp.sum(jnp.where(lane == i2, before, 0.0), axis=-1, keepdims=True)
    cnt_ref[...] = jnp.sum(chosen, axis=0, keepdims=True)
    fields = (i1.astype(F32), i2.astype(F32), r1, r2, w1, w2)
    meta = jnp.zeros_like(logits)
    for k, val in enumerate(fields):
        meta = jnp.where(lane == k, val, meta)
    meta_ref[...] = meta


def _route(x2, gn, router, *, tm):
    n, d = x2.shape
    nt = n // tm
    tri = jnp.tril(jnp.ones((tm, tm), F32), -1).astype(BF16)
    return pl.pallas_call(
        _route_kernel,
        grid=(nt,),
        in_specs=[
            pl.BlockSpec((tm, d), lambda i: (i, 0)),
            pl.BlockSpec((1, d), lambda i: (0, 0)),
            pl.BlockSpec((d, ROUTER_LANES), lambda i: (0, 0)),
            pl.BlockSpec((tm, tm), lambda i: (0, 0)),
        ],
        out_specs=[
            pl.BlockSpec((tm, d), lambda i: (i, 0)),
            pl.BlockSpec((tm, ROUTER_LANES), lambda i: (i, 0)),
            pl.BlockSpec((None, 1, ROUTER_LANES), lambda i: (i, 0, 0)),
        ],
        out_shape=[
            jax.ShapeDtypeStruct((n, d), BF16),
            jax.ShapeDtypeStruct((n, ROUTER_LANES), F32),
            jax.ShapeDtypeStruct((nt, 1, ROUTER_LANES), F32),
        ],
        compiler_params=_params("parallel"),
        name="moe_route",
    )(x2, gn, router, tri)


def _moe_tables(counts, n_row_tiles):
    g = MOE_GROUP_ALIGN
    tf = MOE_ROW_TILE
    cpad = ((counts + g - 1) // g) * g
    tot = cpad.sum(0)
    padded = ((tot + tf - 1) // tf) * tf
    off = jnp.cumsum(padded) - padded
    gstart = off[None] + jnp.cumsum(cpad, axis=0) - cpad
    lend = jnp.cumsum(cpad, axis=1)
    lstart = lend - cpad
    piece_row = jnp.arange(MOE_PIECES, dtype=I32) * g
    piece_e = jnp.sum(piece_row[None, :, None] >= lend[:, None, :], axis=-1)
    e_c = jnp.minimum(piece_e, N_EXPERTS - 1)
    dst = (jnp.take_along_axis(gstart, e_c, axis=1) + piece_row[None]
           - jnp.take_along_axis(lstart, e_c, axis=1))
    piece_dst = jnp.where(piece_e < N_EXPERTS, dst, -1).astype(I32).ravel()
    tile_start = jnp.arange(n_row_tiles, dtype=I32) * tf
    ends = (off + padded).astype(I32)
    tile_valid = (tile_start < ends[-1]).astype(I32)
    tile_expert = jnp.sum(tile_start[:, None] >= ends[None, :], axis=1).astype(I32)
    last_expert = jnp.max(jnp.where(tot > 0, jnp.arange(N_EXPERTS), 0)).astype(I32)
    tile_expert = jnp.where(tile_valid > 0, tile_expert, last_expert)
    return lstart.astype(I32), piece_dst, tile_expert, tile_valid


def _piece_copy(dst_ref, tile, q, buf_ref, rows_ref, sem, *, to_rows):
    g = MOE_GROUP_ALIGN
    dst = pl.multiple_of(jnp.maximum(dst_ref[tile * MOE_PIECES + q], 0), g)
    local = buf_ref.at[pl.ds(pl.multiple_of(q * g, g), g)]
    remote = rows_ref.at[pl.ds(dst, g)]
    return pltpu.make_async_copy(local, remote, sem) if to_rows else pltpu.make_async_copy(remote, local, sem)


def _for_each_piece(dst_ref, tile, fn):
    def body(q, carry):
        @pl.when(dst_ref[tile * MOE_PIECES + q] >= 0)
        def _():
            fn(q)
        return carry
    lax.fori_loop(0, MOE_PIECES, body, 0)


def _dispatch_kernel(dst_ref, hn_ref, l1_ref, l2_ref, xs_in_ref, xs_ref, buf_ref, sem):
    del xs_in_ref
    i = pl.program_id(0)
    row = lax.broadcasted_iota(I32, (buf_ref.shape[0], hn_ref.shape[0]), 0)
    sel = jnp.where(row == l1_ref[...], 1.0, jnp.where(row == l2_ref[...], 1.0, 0.0))
    buf_ref[...] = _dot(sel.astype(BF16), hn_ref[...]).astype(BF16)
    copy = functools.partial(_piece_copy, dst_ref, i, buf_ref=buf_ref, rows_ref=xs_ref, sem=sem, to_rows=True)
    _for_each_piece(dst_ref, i, lambda q: copy(q).start())
    _for_each_piece(dst_ref, i, lambda q: copy(q).wait())


def _dispatch(piece_dst, hn, lrow1, lrow2, n_rows):
    n, d = hn.shape
    t = MOE_TOKEN_TILE
    xs0 = jnp.zeros((n_rows, d), BF16)
    grid_spec = pltpu.PrefetchScalarGridSpec(
        num_scalar_prefetch=1,
        grid=(n // t,),
        in_specs=[
            pl.BlockSpec((t, d), lambda i, dst: (i, 0)),
            pl.BlockSpec((None, 1, t), lambda i, dst: (i, 0, 0)),
            pl.BlockSpec((None, 1, t), lambda i, dst: (i, 0, 0)),
            pl.BlockSpec(memory_space=pl.ANY),
        ],
        out_specs=pl.BlockSpec(memory_space=pl.ANY),
        scratch_shapes=[pltpu.VMEM((MOE_PIECES * MOE_GROUP_ALIGN, d), BF16), pltpu.SemaphoreType.DMA(())],
    )
    return pl.pallas_call(
        _dispatch_kernel,
        grid_spec=grid_spec,
        out_shape=jax.ShapeDtypeStruct((n_rows, d), BF16),
        input_output_aliases={4: 0},
        compiler_params=_params("arbitrary"),
        name="moe_dispatch",
    )(piece_dst, hn, lrow1, lrow2, xs0)


def _experts_kernel(te_ref, tv_ref, xs_ref, wg_ref, wu_ref, wd_ref, ys_ref, acc_ref):
    del te_ref
    j = pl.program_id(0)
    f = pl.program_id(1)
    valid = tv_ref[j] > 0

    @pl.when(f == 0)
    def _():
        acc_ref[...] = jnp.zeros_like(acc_ref)

    @pl.when(valid)
    def _():
        h = xs_ref[...]
        a = _dot(h, wg_ref[...])
        b = _dot(h, wu_ref[...])
        acc_ref[...] += _dot((a * _sigmoid(a) * b).astype(BF16), wd_ref[...])

    @pl.when(f == pl.num_programs(1) - 1)
    def _():
        ys_ref[...] = acc_ref[...].astype(BF16)


def _experts(tile_expert, tile_valid, xs, wg, wu, wd, *, tf):
    n_rows, d = xs.shape
    dff = wg.shape[2]
    tm = MOE_ROW_TILE
    grid_spec = pltpu.PrefetchScalarGridSpec(
        num_scalar_prefetch=2,
        grid=(n_rows // tm, dff // tf),
        in_specs=[
            pl.BlockSpec((tm, d), lambda j, f, te, tv: (j, 0)),
            pl.BlockSpec((None, d, tf), lambda j, f, te, tv: (te[j], 0, f * tv[j])),
            pl.BlockSpec((None, d, tf), lambda j, f, te, tv: (te[j], 0, f * tv[j])),
            pl.BlockSpec((None, tf, d), lambda j, f, te, tv: (te[j], f * tv[j], 0)),
        ],
        out_specs=pl.BlockSpec((tm, d), lambda j, f, te, tv: (j, 0)),
        scratch_shapes=[pltpu.VMEM((tm, d), F32)],
    )
    return pl.pallas_call(
        _experts_kernel,
        grid_spec=grid_spec,
        out_shape=jax.ShapeDtypeStruct((n_rows, d), BF16),
        compiler_params=_params("parallel", "arbitrary"),
        name="moe_experts",
    )(tile_expert, tile_valid, xs, wg, wu, wd)


def _combine_kernel(dst_ref, x_ref, l1_ref, l2_ref, w1_ref, w2_ref, ys_ref, o_ref, buf_ref, sem):
    i = pl.program_id(0)
    g = MOE_GROUP_ALIGN
    copy = functools.partial(_piece_copy, dst_ref, i, buf_ref=buf_ref, rows_ref=ys_ref, sem=sem, to_rows=False)
    _for_each_piece(dst_ref, i, lambda q: copy(q).start())

    def zero_unused(q, carry):
        @pl.when(dst_ref[i * MOE_PIECES + q] < 0)
        def _():
            buf_ref[pl.ds(pl.multiple_of(q * g, g), g), :] = jnp.zeros((g, buf_ref.shape[1]), BF16)
        return carry
    lax.fori_loop(0, MOE_PIECES, zero_unused, 0)

    col = lax.broadcasted_iota(I32, (x_ref.shape[0], buf_ref.shape[0]), 1)
    sel = jnp.where(col == l1_ref[...], w1_ref[...], jnp.where(col == l2_ref[...], w2_ref[...], 0.0))
    _for_each_piece(dst_ref, i, lambda q: copy(q).wait())
    o_ref[...] = x_ref[...] + _dot(sel.astype(BF16), buf_ref[...])


def _combine(piece_dst, x2, ys, lrow1, lrow2, w1, w2):
    n, d = x2.shape
    t = MOE_TOKEN_TILE
    tok = lambda i, dst: (i, 0)
    grid_spec = pltpu.PrefetchScalarGridSpec(
        num_scalar_prefetch=1,
        grid=(n // t,),
        in_specs=[
            pl.BlockSpec((t, d), tok),
            pl.BlockSpec((t, 1), tok),
            pl.BlockSpec((t, 1), tok),
            pl.BlockSpec((t, 1), tok),
            pl.BlockSpec((t, 1), tok),
            pl.BlockSpec(memory_space=pl.ANY),
        ],
        out_specs=pl.BlockSpec((t, d), tok),
        scratch_shapes=[pltpu.VMEM((MOE_PIECES * MOE_GROUP_ALIGN, d), BF16), pltpu.SemaphoreType.DMA(())],
    )
    return pl.pallas_call(
        _combine_kernel,
        grid_spec=grid_spec,
        out_shape=jax.ShapeDtypeStruct((n, d), F32),
        compiler_params=_params("arbitrary"),
        name="moe_combine",
    )(piece_dst, x2, lrow1, lrow2, w1, w2, ys)


def _moe(x2, gn, router, wg, wu, wd):
    n, d = x2.shape
    t = MOE_TOKEN_TILE
    nt = n // t
    n_row_tiles = (2 * n + nt * N_EXPERTS * (MOE_GROUP_ALIGN - 1) + N_EXPERTS * (MOE_ROW_TILE - 1)) // MOE_ROW_TILE
    hn, meta, cnt = _route(x2, gn, router, tm=t)
    counts = cnt[:, 0, :N_EXPERTS].astype(I32)
    lstart, piece_dst, tile_expert, tile_valid = _moe_tables(counts, n_row_tiles)
    lrow1 = jnp.take_along_axis(lstart, meta[:, 0].astype(I32).reshape(nt, t), axis=1) + meta[:, 2].astype(I32).reshape(nt, t)
    lrow2 = jnp.take_along_axis(lstart, meta[:, 1].astype(I32).reshape(nt, t), axis=1) + meta[:, 3].astype(I32).reshape(nt, t)
    xs = _dispatch(piece_dst, hn, lrow1.reshape(nt, 1, t), lrow2.reshape(nt, 1, t), n_row_tiles * MOE_ROW_TILE)
    ys = _experts(tile_expert, tile_valid, xs, wg, wu, wd, tf=512)
    return _combine(piece_dst, x2, ys, lrow1.reshape(n, 1), lrow2.reshape(n, 1), meta[:, 4:5], meta[:, 5:6])


def _qkv_kernel(x_ref, gn_ref, wqk_ref, wvt_ref, seg_ref, gqk_ref, qk_ref, vt_ref):
    h = _rms(x_ref[...], gn_ref[...]).astype(BF16)
    vt_ref[...] = _dot_nt(wvt_ref[...], h).astype(BF16)
    z = _dot(h, wqk_ref[...])
    seg = seg_ref[...]
    sw = seg.shape[0]
    for c in range(z.shape[1] // sw):
        cols = slice(c * sw, (c + 1) * sw)
        zc = z[:, cols]
        ms = _dot((zc * zc).astype(BF16), seg) * (1.0 / ATT_HEAD_DIM)
        qk_ref[:, cols] = (zc * lax.rsqrt(ms + EPS) * gqk_ref[:, cols]).astype(BF16)


def _qkv(x2, gn, wqk, wvt, seg, gqk, *, bsz, tm):
    n, d = x2.shape
    seq = n // bsz
    qk_cols = wqk.shape[1]
    v_cols = wvt.shape[0]
    per_b = seq // tm
    return pl.pallas_call(
        _qkv_kernel,
        grid=(bsz, per_b),
        in_specs=[
            pl.BlockSpec((tm, d), lambda b, i: (b * per_b + i, 0)),
            pl.BlockSpec((1, d), lambda b, i: (0, 0)),
            pl.BlockSpec((d, qk_cols), lambda b, i: (0, 0)),
            pl.BlockSpec((v_cols, d), lambda b, i: (0, 0)),
            pl.BlockSpec(seg.shape, lambda b, i: (0, 0)),
            pl.BlockSpec((1, qk_cols), lambda b, i: (0, 0)),
        ],
        out_specs=[
            pl.BlockSpec((tm, qk_cols), lambda b, i: (b * per_b + i, 0)),
            pl.BlockSpec((None, v_cols, tm), lambda b, i: (b, 0, i)),
        ],
        out_shape=[
            jax.ShapeDtypeStruct((n, qk_cols), BF16),
            jax.ShapeDtypeStruct((bsz, v_cols, seq), BF16),
        ],
        compiler_params=_params("parallel", "parallel"),
        name="od_qkv",
    )(x2, gn, wqk, wvt, seg, gqk)


def _attn_kernel(qi_ref, ki_ref, scal_ref, q_ref, k_ref, vt_ref, sg_ref, o_ref, qm_ref, l_ref, acc_ref, *maybe_m_ref,
                 bq, bk, out_scale, running_max):
    p = pl.program_id(2)
    qi = qi_ref[p]
    ki = ki_ref[p]
    last_k = ((qi + 1) * bq - 1) // bk

    @pl.when(ki == 0)
    def _():
        q = q_ref[...]
        lane = lax.broadcasted_iota(I32, q.shape, 1)
        zero = jnp.zeros_like(q)
        qm_ref[0] = jnp.where(lane < ATT_HEAD_DIM, q, zero)
        qm_ref[1] = jnp.where(lane >= ATT_HEAD_DIM, q, zero)
        l_ref[...] = jnp.zeros_like(l_ref)
        acc_ref[...] = jnp.zeros_like(acc_ref)
        if running_max:
            maybe_m_ref[0][...] = jnp.full_like(maybe_m_ref[0], -jnp.inf)

    def step(masked):
        k = k_ref[...]
        vt = vt_ref[...]
        if masked:
            kpos = ki * bk + lax.broadcasted_iota(I32, (bk, bq), 0)
            qpos = qi * bq + lax.broadcasted_iota(I32, (bk, bq), 1)
            keep = kpos <= qpos
        for m in range(2):
            s = _dot_nt(k, qm_ref[m])
            if masked:
                s = jnp.where(keep, s, -jnp.inf)
            if running_max:
                m_ref = maybe_m_ref[0]
                m_old = m_ref[m]
                m_new = jnp.maximum(m_old, jnp.max(s, axis=0, keepdims=True))
                pt = jnp.exp2(s - m_new)
                alpha = jnp.exp2(m_old - m_new)
                l_ref[m] = alpha * l_ref[m] + jnp.sum(pt, axis=0, keepdims=True)
                acc_ref[m] = alpha * acc_ref[m] + _dot(vt, pt.astype(BF16))
                m_ref[m] = m_new
            else:
                pt = jnp.exp2(s)
                l_ref[m] += jnp.sum(pt, axis=0, keepdims=True)
                acc_ref[m] += _dot(vt, pt.astype(BF16))

    def step_plain(masked):
        s = _dot_nt(k_ref[...], qm_ref[...].reshape(2 * bq, qm_ref.shape[-1]))
        if masked:
            kpos = ki * bk + lax.broadcasted_iota(I32, s.shape, 0)
            qcol = lax.broadcasted_iota(I32, s.shape, 1)
            qpos = qi * bq + jnp.where(qcol >= bq, qcol - bq, qcol)
            s = jnp.where(kpos <= qpos, s, -jnp.inf)
        pt = jnp.exp2(s)
        lsum = jnp.sum(pt, axis=0, keepdims=True)
        pv = _dot(vt_ref[...], pt.astype(BF16))
        for m in range(2):
            l_ref[m] += lsum[:, m * bq:(m + 1) * bq]
            acc_ref[m] += pv[:, m * bq:(m + 1) * bq]

    if not running_max:
        step = step_plain

    fully_visible = (ki + 1) * bk - 1 <= qi * bq

    @pl.when(fully_visible)
    def _():
        step(False)

    @pl.when(jnp.logical_not(fully_visible))
    def _():
        step(True)

    @pl.when(ki == last_k)
    def _():
        o = acc_ref[0] / l_ref[0] - scal_ref[0] * (acc_ref[1] / l_ref[1])
        ms = jnp.mean(o * o, axis=0, keepdims=True)
        o = o * lax.rsqrt(ms + EPS) * (sg_ref[...] * out_scale)
        o_ref[...] = o.T.astype(BF16)


def _attention(qk3, vt3, scal, subln_col, *, bq, bk, lambda_init, running_max):
    bsz, seq, _ = qk3.shape
    dv = 2 * ATT_HEAD_DIM
    nh = ATT_HEADS
    pairs = [(qi, ki) for qi in range(seq // bq) for ki in range(((qi + 1) * bq - 1) // bk + 1)]
    qi_tab = jnp.asarray(np.array([p[0] for p in pairs], np.int32))
    ki_tab = jnp.asarray(np.array([p[1] for p in pairs], np.int32))
    kern = functools.partial(_attn_kernel, bq=bq, bk=bk, out_scale=1.0 - lambda_init, running_max=running_max)
    scratch = [
        pltpu.VMEM((2, bq, dv), BF16),
        pltpu.VMEM((2, 1, bq), F32),
        pltpu.VMEM((2, dv, bq), F32),
    ]
    if running_max:
        scratch.append(pltpu.VMEM((2, 1, bq), F32))
    grid_spec = pltpu.PrefetchScalarGridSpec(
        num_scalar_prefetch=2,
        grid=(bsz, nh, len(pairs)),
        in_specs=[
            pl.BlockSpec(memory_space=pltpu.SMEM),
            pl.BlockSpec((None, bq, dv), lambda b, h, p, qt, kt: (b, qt[p], h)),
            pl.BlockSpec((None, bk, dv), lambda b, h, p, qt, kt: (b, kt[p], nh + h)),
            pl.BlockSpec((None, dv, bk), lambda b, h, p, qt, kt: (b, h, kt[p])),
            pl.BlockSpec((dv, 1), lambda b, h, p, qt, kt: (0, 0)),
        ],
        out_specs=pl.BlockSpec((None, bq, dv), lambda b, h, p, qt, kt: (b, qt[p], h)),
        scratch_shapes=scratch,
    )
    return pl.pallas_call(
        kern,
        grid_spec=grid_spec,
        out_shape=jax.ShapeDtypeStruct((bsz, seq, nh * dv), BF16),
        compiler_params=_params("parallel", "parallel", "arbitrary"),
        name="od_attn_online" if running_max else "od_attn",
    )(qi_tab, ki_tab, scal, qk3, qk3, vt3, subln_col)


def _proj_res_kernel(x_ref, y_ref, w_ref, o_ref):
    o_ref[...] = x_ref[...] + _dot(y_ref[...], w_ref[...])


def _proj_res(x2, y, w, *, tm):
    n, d = x2.shape
    k = y.shape[1]
    return pl.pallas_call(
        _proj_res_kernel,
        grid=(n // tm,),
        in_specs=[
            pl.BlockSpec((tm, d), lambda i: (i, 0)),
            pl.BlockSpec((tm, k), lambda i: (i, 0)),
            pl.BlockSpec((k, d), lambda i: (0, 0)),
        ],
        out_specs=pl.BlockSpec((tm, d), lambda i: (i, 0)),
        out_shape=jax.ShapeDtypeStruct((n, d), F32),
        compiler_params=_params("parallel"),
        name="od_out",
    )(x2, y, w)


def _row(v):
    return v.astype(F32).reshape(1, -1)


def _pick(n, pref):
    t = min(n, pref)
    assert n % t == 0, (n, pref)
    return t


def kernel(x, ev_norm_mix, ev_w_in, ev_gmlp_ln_g, ev_gmlp_ln_b, ev_gmlp_ws, ev_gmlp_bs, ev_s5_lambda_re, ev_s5_lambda_im, ev_s5_log_dt, ev_s5_b_re, ev_s5_b_im, ev_s5_c_re, ev_s5_c_im, ev_s5_d, ev_s5_w_glu, ev_s5_b_glu, ev_w_out, ev_norm_ffn, ev_ffn_w_gate, ev_ffn_w_up, ev_ffn_w_down, od_norm_mix, od_w_in, od_q_norm, od_k_norm, od_lambda_q1, od_lambda_k1, od_lambda_q2, od_lambda_k2, od_subln, od_w_out, od_norm_ffn, od_router, od_moe_w_gate, od_moe_w_up, od_moe_w_down):
    bsz, seq, d = x.shape
    n = bsz * seq
    depth = ev_norm_mix.shape[0] + od_norm_mix.shape[0]
    a_width = ev_gmlp_ln_g.shape[-1]
    assert seq % (S5_BLOCK * 2) == 0 and seq % GMLP_CHUNK == 0 and n % MOE_TOKEN_TILE == 0
    r = seq // S5_BLOCK
    nsteps = max(1, (r - 1).bit_length())

    tm_in = _pick(seq, 512)
    bq = _pick(seq, 1024)
    bk = _pick(seq, 1024)
    qk_cols = 2 * ATT_HEADS * 2 * ATT_HEAD_DIM
    tril = jnp.tril(jnp.ones((GMLP_CHUNK, GMLP_CHUNK), F32))
    seg_id = jnp.arange(4 * ATT_HEAD_DIM) // ATT_HEAD_DIM
    seg = (seg_id[:, None] == seg_id[None, :]).astype(BF16)

    x2 = x.reshape(n, d)
    for layer in range(depth):
        i = layer // 2
        if layer % 2 == 0:
            wm = (ev_gmlp_ws[i].astype(F32) * tril).astype(BF16)
            bias = jnp.repeat(ev_gmlp_bs[i].astype(F32).T, a_width // A_HEADS, axis=1)
            ya, zb = _ev_in(x2, _row(ev_norm_mix[i]), ev_w_in[i].astype(BF16), _row(ev_gmlp_ln_g[i]),
                            _row(ev_gmlp_ln_b[i]), wm, bias, tm=tm_in)
            bw = zb.shape[1]
            zbt = zb.reshape(bsz, r, S5_BLOCK, bw).transpose(0, 2, 3, 1)
            ops = _s5_operators(ev_s5_lambda_re[i], ev_s5_lambda_im[i], ev_s5_log_dt[i], ev_s5_b_re[i],
                                ev_s5_b_im[i], ev_s5_c_re[i], ev_s5_c_im[i], ev_s5_d[i], nsteps)
            ybt = _s5(zbt, *ops)
            yb = ybt.transpose(0, 3, 1, 2).reshape(n, bw)
            w_out = ev_w_out[i].astype(BF16)
            x2 = _ev_out(x2, ya, yb, ev_s5_w_glu[i].astype(BF16), _row(ev_s5_b_glu[i]),
                         w_out[:a_width], w_out[a_width:], tm=tm_in)
            x2 = _ffn(x2, _row(ev_norm_ffn[i]), ev_ffn_w_gate[i].astype(BF16), ev_ffn_w_up[i].astype(BF16),
                      ev_ffn_w_down[i].astype(BF16), tm=tm_in, tf=ev_ffn_w_gate.shape[-1] // 2)
        else:
            lambda_init = 0.8 - 0.6 * math.exp(-0.3 * layer)
            gqk = jnp.concatenate([
                jnp.tile(od_q_norm[i].astype(F32) * (ATT_HEAD_DIM ** -0.5 * LOG2E), 2 * ATT_HEADS),
                jnp.tile(od_k_norm[i].astype(F32), 2 * ATT_HEADS)]).reshape(1, -1)
            w_in = od_w_in[i].astype(BF16)
            qk, vt = _qkv(x2, _row(od_norm_mix[i]), w_in[:, :qk_cols], w_in[:, qk_cols:].T, seg, gqk,
                          bsz=bsz, tm=tm_in)
            lam = (jnp.exp(jnp.sum(od_lambda_q1[i].astype(F32) * od_lambda_k1[i].astype(F32)))
                   - jnp.exp(jnp.sum(od_lambda_q2[i].astype(F32) * od_lambda_k2[i].astype(F32)))
                   + lambda_init)
            bound = (ATT_HEAD_DIM * LOG2E * ATT_HEAD_DIM ** -0.5
                     * jnp.max(jnp.abs(od_q_norm[i].astype(F32))) * jnp.max(jnp.abs(od_k_norm[i].astype(F32))))
            scal = jnp.stack([lam, bound])
            attend = functools.partial(_attention, bq=bq, bk=bk, lambda_init=lambda_init)
            yc = lax.cond(bound <= ATT_STATIC_MAX_LIMIT,
                          functools.partial(attend, running_max=False),
                          functools.partial(attend, running_max=True),
                          qk.reshape(bsz, seq, qk_cols), vt, scal, od_subln[i].astype(F32).reshape(-1, 1))
            x2 = _proj_res(x2, yc.reshape(n, -1), od_w_out[i].astype(BF16), tm=tm_in)
            router = jnp.pad(od_router[i].astype(F32), ((0, 0), (0, ROUTER_LANES - N_EXPERTS)))
            x2 = _moe(x2, _row(od_norm_ffn[i]), router, od_moe_w_gate[i].astype(BF16),
                      od_moe_w_up[i].astype(BF16), od_moe_w_down[i].astype(BF16))
    return x2.reshape(bsz, seq, d)
```

```python
import functools
import math

import numpy as np
import jax
import jax.numpy as jnp
from jax import lax
from jax.experimental import pallas as pl
from jax.experimental.pallas import tpu as pltpu

F32 = jnp.float32
BF16 = jnp.bfloat16
I32 = jnp.int32

EPS = 1e-6
A_HEADS = 4
GMLP_CHUNK = 128
S5_GROUP = 16
S5_STATE = 64
S5_BLOCK = 16
DT_LAMBDA_RE_MAX = -1e-4
ATT_HEADS = 8
ATT_HEAD_DIM = 64
N_EXPERTS = 8
ROUTER_LANES = 128
MOE_TOKEN_TILE = 512
MOE_ROW_TILE = 1024
MOE_GROUP_ALIGN = 16
MOE_PIECES = -(-(2 * MOE_TOKEN_TILE + N_EXPERTS * (MOE_GROUP_ALIGN - 1)) // MOE_GROUP_ALIGN)
LOG2E = 1.4426950408889634
ATT_STATIC_MAX_LIMIT = 40.0
VMEM_LIMIT = 56 * 1024 * 1024


def _params(*sem):
    return pltpu.CompilerParams(dimension_semantics=sem, vmem_limit_bytes=VMEM_LIMIT)


def _rms(x, g):
    return x * lax.rsqrt(jnp.mean(x * x, axis=-1, keepdims=True) + EPS) * g


def _gelu(x):
    return 0.5 * x * (1.0 + jnp.tanh(0.7978845608028654 * (x + 0.044715 * (x * x * x))))


def _sigmoid(x):
    return 1.0 / (1.0 + jnp.exp(-x))


def _dot(a, b):
    return jnp.dot(a, b, preferred_element_type=F32)


def _dot_nt(a, b):
    return lax.dot_general(a, b, (((1,), (1,)), ((), ())), preferred_element_type=F32)


def _ev_in_kernel(x_ref, gn_ref, w_ref, lng_ref, lnb_ref, wm_ref, bias_ref, ya_ref, zb_ref, *, n_chunk, a_width):
    h = _rms(x_ref[...], gn_ref[...]).astype(BF16)
    z = _dot(h, w_ref[...])
    zb_ref[...] = z[:, 2 * a_width:].astype(BF16)
    g = _gelu(z[:, :2 * a_width])
    u = g[:, :a_width]
    v = g[:, a_width:]
    mu = jnp.mean(v, axis=-1, keepdims=True)
    vc = v - mu
    vn = vc * lax.rsqrt(jnp.mean(vc * vc, axis=-1, keepdims=True) + EPS) * lng_ref[...] + lnb_ref[...]
    vb = vn.astype(BF16)
    hd = a_width // A_HEADS
    for c in range(n_chunk):
        rows = slice(c * GMLP_CHUNK, (c + 1) * GMLP_CHUNK)
        for k in range(A_HEADS):
            cols = slice(k * hd, (k + 1) * hd)
            s = _dot(wm_ref[k], vb[rows, cols]) + bias_ref[:, cols]
            ya_ref[rows, cols] = (u[rows, cols] * s).astype(BF16)


def _ev_in(x2, gn, w_in, ln_g, ln_b, wm, bias, *, tm):
    n, d = x2.shape
    cols = w_in.shape[1]
    a_width = ln_g.shape[-1]
    b_width = cols - 2 * a_width
    kern = functools.partial(_ev_in_kernel, n_chunk=tm // GMLP_CHUNK, a_width=a_width)
    return pl.pallas_call(
        kern,
        grid=(n // tm,),
        in_specs=[
            pl.BlockSpec((tm, d), lambda i: (i, 0)),
            pl.BlockSpec((1, d), lambda i: (0, 0)),
            pl.BlockSpec((d, cols), lambda i: (0, 0)),
            pl.BlockSpec((1, a_width), lambda i: (0, 0)),
            pl.BlockSpec((1, a_width), lambda i: (0, 0)),
            pl.BlockSpec((A_HEADS, GMLP_CHUNK, GMLP_CHUNK), lambda i: (0, 0, 0)),
            pl.BlockSpec((GMLP_CHUNK, a_width), lambda i: (0, 0)),
        ],
        out_specs=[
            pl.BlockSpec((tm, a_width), lambda i: (i, 0)),
            pl.BlockSpec((tm, b_width), lambda i: (i, 0)),
        ],
        out_shape=[
            jax.ShapeDtypeStruct((n, a_width), BF16),
            jax.ShapeDtypeStruct((n, b_width), BF16),
        ],
        compiler_params=_params("parallel"),
        name="ev_in_gmlp",
    )(x2, gn, w_in, ln_g, ln_b, wm, bias)


def _s5_kernel(u_ref, tt_ref, pt_ref, qt_ref, ar_ref, ai_ref, o_ref, *, nb, nsteps):
    r = u_ref.shape[-1]
    rows = S5_BLOCK * S5_GROUP
    lane = lax.broadcasted_iota(jnp.int32, (S5_STATE, r), 1)

    def shifted(a, sh):
        return jnp.where(lane >= sh, pltpu.roll(a, sh, 1), 0.0)

    for b in range(nb):
        ut = u_ref[b].reshape(rows, r)
        st = _dot(pt_ref[...], ut)
        xr = st[:S5_STATE]
        xi = st[S5_STATE:]
        for j in range(nsteps):
            sh = 1 << j
            ar = ar_ref[j]
            ai = ai_ref[j]
            sr = shifted(xr, sh)
            si = shifted(xi, sh)
            xr, xi = xr + ar * sr - ai * si, xi + ar * si + ai * sr
        xprev = jnp.concatenate([shifted(xr, 1), shifted(xi, 1)], axis=0).astype(BF16)
        y = _dot(tt_ref[...], ut) + _dot(qt_ref[...], xprev)
        o_ref[b] = _gelu(y).astype(BF16).reshape(S5_BLOCK, S5_GROUP, r)


def _s5(zbt, tt, pt, qt, ar, ai):
    bsz, _, width, r = zbt.shape
    groups = width // S5_GROUP
    nsteps = ar.shape[1]
    u5 = zbt.reshape(bsz, S5_BLOCK, groups, S5_GROUP, r)
    rows = S5_BLOCK * S5_GROUP
    kern = functools.partial(_s5_kernel, nb=bsz, nsteps=nsteps)
    out = pl.pallas_call(
        kern,
        grid=(groups,),
        in_specs=[
            pl.BlockSpec((bsz, S5_BLOCK, None, S5_GROUP, r), lambda g: (0, 0, g, 0, 0)),
            pl.BlockSpec((None, rows, rows), lambda g: (g, 0, 0)),
            pl.BlockSpec((None, 2 * S5_STATE, rows), lambda g: (g, 0, 0)),
            pl.BlockSpec((None, rows, 2 * S5_STATE), lambda g: (g, 0, 0)),
            pl.BlockSpec((None, nsteps, S5_STATE, 1), lambda g: (g, 0, 0, 0)),
            pl.BlockSpec((None, nsteps, S5_STATE, 1), lambda g: (g, 0, 0, 0)),
        ],
        out_specs=pl.BlockSpec((bsz, S5_BLOCK, None, S5_GROUP, r), lambda g: (0, 0, g, 0, 0)),
        out_shape=jax.ShapeDtypeStruct(u5.shape, BF16),
        compiler_params=_params("parallel"),
        name="ev_s5",
    )(u5, tt, pt, qt, ar, ai)
    return out.reshape(zbt.shape)


def _s5_operators(lam_re, lam_im, log_dt, b_re, b_im, c_re, c_im, d, nsteps):
    groups, p = lam_re.shape
    c = S5_GROUP
    ln = S5_BLOCK
    lr = jnp.minimum(lam_re.astype(F32), DT_LAMBDA_RE_MAX)
    li = lam_im.astype(F32)
    dt = jnp.exp(log_dt.astype(F32))[:, None]
    zr = lr * dt
    zi = li * dt

    def powers(k):
        kk = jnp.asarray(k, F32)[None, :, None]
        mag = jnp.exp(zr[:, None, :] * kk)
        ang = zi[:, None, :] * kk
        return mag * jnp.cos(ang), mag * jnp.sin(ang)

    lbr, lbi = powers(jnp.ones((1,)))
    lbr = lbr[:, 0]
    lbi = lbi[:, 0]
    den = lr * lr + li * li
    fr = ((lbr - 1.0) * lr + lbi * li) / den
    fi = (lbi * lr - (lbr - 1.0) * li) / den
    bbr = fr[..., None] * b_re.astype(F32) - fi[..., None] * b_im.astype(F32)
    bbi = fr[..., None] * b_im.astype(F32) + fi[..., None] * b_re.astype(F32)
    cr = c_re.astype(F32)
    ci = c_im.astype(F32)

    pwr, pwi = powers(jnp.arange(ln + 1))
    mr = pwr[:, :ln, :, None] * bbr[:, None] - pwi[:, :ln, :, None] * bbi[:, None]
    mi = pwr[:, :ln, :, None] * bbi[:, None] + pwi[:, :ln, :, None] * bbr[:, None]
    kmat = jnp.einsum("gcp,gkpd->gkcd", cr, mr) - jnp.einsum("gcp,gkpd->gkcd", ci, mi)
    s_idx = jnp.arange(ln)[:, None]
    t_idx = jnp.arange(ln)[None, :]
    lag = t_idx - s_idx
    blocks = kmat[:, jnp.clip(lag, 0, ln - 1)]
    blocks = jnp.where((lag >= 0)[None, :, :, None, None], blocks, 0.0)
    skip = (jnp.eye(ln, dtype=F32)[None, :, :, None, None]
            * jnp.eye(c, dtype=F32)[None, None, None]
            * d.astype(F32).reshape(groups, 1, 1, c, 1))
    blocks = blocks + skip
    tt = blocks.transpose(0, 2, 3, 1, 4).reshape(groups, ln * c, ln * c)

    rev = ln - 1 - jnp.arange(ln)
    pr = pwr[:, rev][:, :, :, None] * bbr[:, None] - pwi[:, rev][:, :, :, None] * bbi[:, None]
    pi = pwr[:, rev][:, :, :, None] * bbi[:, None] + pwi[:, rev][:, :, :, None] * bbr[:, None]
    pt = jnp.concatenate([pr, pi], axis=2).transpose(0, 2, 1, 3).reshape(groups, 2 * p, ln * c)

    a_r = cr[:, None] * pwr[:, 1:ln + 1, None, :] - ci[:, None] * pwi[:, 1:ln + 1, None, :]
    a_i = cr[:, None] * pwi[:, 1:ln + 1, None, :] + ci[:, None] * pwr[:, 1:ln + 1, None, :]
    qt = jnp.concatenate([a_r, -a_i], axis=-1).reshape(groups, ln * c, 2 * p)

    sr, si = powers(ln * (2 ** jnp.arange(nsteps)))
    return (tt.astype(BF16), pt.astype(BF16), qt.astype(BF16), sr[..., None], si[..., None])


def _ev_out_ffn_kernel(x_ref, ya_ref, yb_ref, wglu_ref, bglu_ref, woa_ref, wob_ref, gn_ref, wg_ref, wu_ref, wd_ref,
                       o_ref, xn_ref, hn_ref, acc_ref):
    f = pl.program_id(1)

    @pl.when(f == 0)
    def _():
        yb = yb_ref[...]
        gate = _sigmoid(_dot(yb, wglu_ref[...]) + bglu_ref[...])
        y2 = (yb.astype(F32) * gate).astype(BF16)
        xn = x_ref[...] + _dot(ya_ref[...], woa_ref[...]) + _dot(y2, wob_ref[...])
        xn_ref[...] = xn
        hn_ref[...] = _rms(xn, gn_ref[...]).astype(BF16)
        acc_ref[...] = jnp.zeros_like(acc_ref)

    h = hn_ref[...]
    a = _dot(h, wg_ref[...])
    b = _dot(h, wu_ref[...])
    acc_ref[...] += _dot((a * _sigmoid(a) * b).astype(BF16), wd_ref[...])

    @pl.when(f == pl.num_programs(1) - 1)
    def _():
        o_ref[...] = xn_ref[...] + acc_ref[...]


def _ev_out_ffn(x2, ya, yb, wglu, bglu, woa, wob, gn, wg, wu, wd, *, tm, tf):
    n, d = x2.shape
    aw = ya.shape[1]
    bw = yb.shape[1]
    dff = wg.shape[1]
    tok = lambda i, f: (i, 0)
    const = lambda i, f: (0, 0)
    return pl.pallas_call(
        _ev_out_ffn_kernel,
        grid=(n // tm, dff // tf),
        in_specs=[
            pl.BlockSpec((tm, d), tok),
            pl.BlockSpec((tm, aw), tok),
            pl.BlockSpec((tm, bw), tok),
            pl.BlockSpec((bw, bw), const),
            pl.BlockSpec((1, bw), const),
            pl.BlockSpec((aw, d), const),
            pl.BlockSpec((bw, d), const),
            pl.BlockSpec((1, d), const),
            pl.BlockSpec((d, tf), lambda i, f: (0, f)),
            pl.BlockSpec((d, tf), lambda i, f: (0, f)),
            pl.BlockSpec((tf, d), lambda i, f: (f, 0)),
        ],
        out_specs=pl.BlockSpec((tm, d), tok),
        out_shape=jax.ShapeDtypeStruct((n, d), F32),
        scratch_shapes=[pltpu.VMEM((tm, d), F32), pltpu.VMEM((tm, d), BF16), pltpu.VMEM((tm, d), F32)],
        compiler_params=_params("parallel", "arbitrary"),
        name="ev_out_ffn",
    )(x2, ya, yb, wglu, bglu, woa, wob, gn, wg, wu, wd)


def _route_kernel(x_ref, y_ref, wo_ref, gn_ref, router_ref, tri_ref, xn_ref, hn_ref, meta_ref, cnt_ref):
    xn = x_ref[...] + _dot(y_ref[...], wo_ref[...])
    xn_ref[...] = xn
    hn = _rms(xn, gn_ref[...])
    hn_ref[...] = hn.astype(BF16)
    logits = jnp.dot(hn, router_ref[...], preferred_element_type=F32, precision=lax.Precision.HIGHEST)
    lane = lax.broadcasted_iota(I32, logits.shape, 1)
    neg = jnp.float32(-jnp.inf)
    logits = jnp.where(lane < N_EXPERTS, logits, neg)
    m1 = jnp.max(logits, axis=-1, keepdims=True)
    i1 = jnp.min(jnp.where(logits == m1, lane, ROUTER_LANES), axis=-1, keepdims=True)
    rest = jnp.where(lane == i1, neg, logits)
    m2 = jnp.max(rest, axis=-1, keepdims=True)
    i2 = jnp.min(jnp.where(rest == m2, lane, ROUTER_LANES), axis=-1, keepdims=True)
    e2 = jnp.exp(m2 - m1)
    w1 = 1.0 / (1.0 + e2)
    w2 = e2 / (1.0 + e2)
    chosen = jnp.where(lane == i1, 1.0, jnp.where(lane == i2, 1.0, 0.0))
    before = _dot(tri_ref[...], chosen.astype(BF16))
    cnt = jnp.sum(chosen, axis=0, keepdims=True)
    cnt_ref[...] = cnt
    cpad = jnp.ceil(cnt * (1.0 / MOE_GROUP_ALIGN)) * MOE_GROUP_ALIGN
    e_row = lax.broadcasted_iota(I32, (ROUTER_LANES, ROUTER_LANES), 0)
    e_col = lax.broadcasted_iota(I32, (ROUTER_LANES, ROUTER_LANES), 1)
    lstart = _dot(cpad.astype(BF16), jnp.where(e_row < e_col, 1.0, 0.0).astype(BF16))
    row_of = before + lstart
    r1 = jnp.sum(jnp.where(lane == i1, row_of, 0.0), axis=-1, keepdims=True)
    r2 = jnp.sum(jnp.where(lane == i2, row_of, 0.0), axis=-1, keepdims=True)
    fields = (r1, r2, w1, w2)
    meta = jnp.zeros_like(logits)
    for k, val in enumerate(fields):
        meta = jnp.where(lane == k, val, meta)
    meta_ref[...] = meta


def _route(x2, y, wo, gn, router, *, tm):
    n, d = x2.shape
    nt = n // tm
    tri = jnp.tril(jnp.ones((tm, tm), F32), -1).astype(BF16)
    return pl.pallas_call(
        _route_kernel,
        grid=(nt,),
        in_specs=[
            pl.BlockSpec((tm, d), lambda i: (i, 0)),
            pl.BlockSpec((tm, y.shape[1]), lambda i: (i, 0)),
            pl.BlockSpec(wo.shape, lambda i: (0, 0)),
            pl.BlockSpec((1, d), lambda i: (0, 0)),
            pl.BlockSpec((d, ROUTER_LANES), lambda i: (0, 0)),
            pl.BlockSpec((tm, tm), lambda i: (0, 0)),
        ],
        out_specs=[
            pl.BlockSpec((tm, d), lambda i: (i, 0)),
            pl.BlockSpec((tm, d), lambda i: (i, 0)),
            pl.BlockSpec((tm, ROUTER_LANES), lambda i: (i, 0)),
            pl.BlockSpec((None, 1, ROUTER_LANES), lambda i: (i, 0, 0)),
        ],
        out_shape=[
            jax.ShapeDtypeStruct((n, d), F32),
            jax.ShapeDtypeStruct((n, d), BF16),
            jax.ShapeDtypeStruct((n, ROUTER_LANES), F32),
            jax.ShapeDtypeStruct((nt, 1, ROUTER_LANES), F32),
        ],
        compiler_params=_params("parallel"),
        name="moe_route",
    )(x2, y, wo, gn, router, tri)


def _moe_tables(counts, n_row_tiles):
    g = MOE_GROUP_ALIGN
    tf = MOE_ROW_TILE
    cpad = ((counts + g - 1) // g) * g
    tot = cpad.sum(0)
    padded = ((tot + tf - 1) // tf) * tf
    off = jnp.cumsum(padded) - padded
    gstart = off[None] + jnp.cumsum(cpad, axis=0) - cpad
    lend = jnp.cumsum(cpad, axis=1)
    lstart = lend - cpad
    piece_row = jnp.arange(MOE_PIECES, dtype=I32) * g
    piece_e = jnp.sum(piece_row[None, :, None] >= lend[:, None, :], axis=-1)
    of_expert = piece_e[:, :, None] == jnp.arange(N_EXPERTS, dtype=I32)[None, None, :]
    dst = piece_row[None] + jnp.sum(jnp.where(of_expert, (gstart - lstart)[:, None, :], 0), axis=-1)
    piece_dst = jnp.where(piece_e < N_EXPERTS, dst, -1).astype(I32).ravel()
    tile_start = jnp.arange(n_row_tiles, dtype=I32) * tf
    ends = (off + padded).astype(I32)
    tile_valid = (tile_start < ends[-1]).astype(I32)
    tile_expert = jnp.sum(tile_start[:, None] >= ends[None, :], axis=1).astype(I32)
    last_expert = jnp.max(jnp.where(tot > 0, jnp.arange(N_EXPERTS), 0)).astype(I32)
    tile_expert = jnp.where(tile_valid > 0, tile_expert, last_expert)
    return piece_dst, tile_expert, tile_valid


def _piece_copy(dst_ref, tile, q, buf_ref, rows_ref, sem, *, to_rows):
    g = MOE_GROUP_ALIGN
    dst = pl.multiple_of(jnp.maximum(dst_ref[tile * MOE_PIECES + q], 0), g)
    local = buf_ref.at[pl.ds(pl.multiple_of(q * g, g), g)]
    remote = rows_ref.at[pl.ds(dst, g)]
    return pltpu.make_async_copy(local, remote, sem) if to_rows else pltpu.make_async_copy(remote, local, sem)


def _for_each_piece(dst_ref, tile, fn):
    def body(q, carry):
        @pl.when(dst_ref[tile * MOE_PIECES + q] >= 0)
        def _():
            fn(q)
        return carry
    lax.fori_loop(0, MOE_PIECES, body, 0)


def _dispatch_kernel(dst_ref, hn_ref, l1_ref, l2_ref, xs_in_ref, xs_ref, buf_ref, sem):
    del xs_in_ref
    i = pl.program_id(0)
    row = lax.broadcasted_iota(I32, (buf_ref.shape[0], hn_ref.shape[0]), 0)
    sel = jnp.where(row == l1_ref[...], 1.0, jnp.where(row == l2_ref[...], 1.0, 0.0))
    buf_ref[...] = _dot(sel.astype(BF16), hn_ref[...]).astype(BF16)
    copy = functools.partial(_piece_copy, dst_ref, i, buf_ref=buf_ref, rows_ref=xs_ref, sem=sem, to_rows=True)
    _for_each_piece(dst_ref, i, lambda q: copy(q).start())
    _for_each_piece(dst_ref, i, lambda q: copy(q).wait())


def _dispatch(piece_dst, hn, lrow1, lrow2, n_rows):
    n, d = hn.shape
    t = MOE_TOKEN_TILE
    xs0 = jnp.zeros((n_rows, d), BF16)
    grid_spec = pltpu.PrefetchScalarGridSpec(
        num_scalar_prefetch=1,
        grid=(n // t,),
        in_specs=[
            pl.BlockSpec((t, d), lambda i, dst: (i, 0)),
            pl.BlockSpec((None, 1, t), lambda i, dst: (i, 0, 0)),
            pl.BlockSpec((None, 1, t), lambda i, dst: (i, 0, 0)),
            pl.BlockSpec(memory_space=pl.ANY),
        ],
        out_specs=pl.BlockSpec(memory_space=pl.ANY),
        scratch_shapes=[pltpu.VMEM((MOE_PIECES * MOE_GROUP_ALIGN, d), BF16), pltpu.SemaphoreType.DMA(())],
    )
    return pl.pallas_call(
        _dispatch_kernel,
        grid_spec=grid_spec,
        out_shape=jax.ShapeDtypeStruct((n_rows, d), BF16),
        input_output_aliases={4: 0},
        compiler_params=_params("arbitrary"),
        name="moe_dispatch",
    )(piece_dst, hn, lrow1, lrow2, xs0)


def _experts_kernel(te_ref, tv_ref, xs_ref, wg_ref, wu_ref, wd_ref, ys_ref, acc_ref):
    del te_ref
    j = pl.program_id(0)
    f = pl.program_id(1)
    valid = tv_ref[j] > 0

    @pl.when(f == 0)
    def _():
        acc_ref[...] = jnp.zeros_like(acc_ref)

    @pl.when(valid)
    def _():
        h = xs_ref[...]
        a = _dot(h, wg_ref[...])
        b = _dot(h, wu_ref[...])
        acc_ref[...] += _dot((a * _sigmoid(a) * b).astype(BF16), wd_ref[...])

    @pl.when(f == pl.num_programs(1) - 1)
    def _():
        ys_ref[...] = acc_ref[...].astype(BF16)


def _experts(tile_expert, tile_valid, xs, wg, wu, wd, *, layer, tf):
    n_rows, d = xs.shape
    dff = wg.shape[3]
    tm = MOE_ROW_TILE
    grid_spec = pltpu.PrefetchScalarGridSpec(
        num_scalar_prefetch=2,
        grid=(n_rows // tm, dff // tf),
        in_specs=[
            pl.BlockSpec((tm, d), lambda j, f, te, tv: (j, 0)),
            pl.BlockSpec((None, None, d, tf), lambda j, f, te, tv: (layer, te[j], 0, f * tv[j])),
            pl.BlockSpec((None, None, d, tf), lambda j, f, te, tv: (layer, te[j], 0, f * tv[j])),
            pl.BlockSpec((None, None, tf, d), lambda j, f, te, tv: (layer, te[j], f * tv[j], 0)),
        ],
        out_specs=pl.BlockSpec((tm, d), lambda j, f, te, tv: (j, 0)),
        scratch_shapes=[pltpu.VMEM((tm, d), F32)],
    )
    return pl.pallas_call(
        _experts_kernel,
        grid_spec=grid_spec,
        out_shape=jax.ShapeDtypeStruct((n_rows, d), BF16),
        compiler_params=_params("parallel", "arbitrary"),
        name="moe_experts",
    )(tile_expert, tile_valid, xs, wg, wu, wd)


def _combine_kernel(dst_ref, x_ref, l1_ref, l2_ref, w1_ref, w2_ref, ys_ref, o_ref, buf_ref, sem):
    i = pl.program_id(0)
    g = MOE_GROUP_ALIGN
    copy = functools.partial(_piece_copy, dst_ref, i, buf_ref=buf_ref, rows_ref=ys_ref, sem=sem, to_rows=False)
    _for_each_piece(dst_ref, i, lambda q: copy(q).start())

    def zero_unused(q, carry):
        @pl.when(dst_ref[i * MOE_PIECES + q] < 0)
        def _():
            buf_ref[pl.ds(pl.multiple_of(q * g, g), g), :] = jnp.zeros((g, buf_ref.shape[1]), BF16)
        return carry
    lax.fori_loop(0, MOE_PIECES, zero_unused, 0)

    col = lax.broadcasted_iota(I32, (x_ref.shape[0], buf_ref.shape[0]), 1)
    sel = jnp.where(col == l1_ref[...], w1_ref[...], jnp.where(col == l2_ref[...], w2_ref[...], 0.0))
    _for_each_piece(dst_ref, i, lambda q: copy(q).wait())
    o_ref[...] = x_ref[...] + _dot(sel.astype(BF16), buf_ref[...])


def _combine(piece_dst, x2, ys, lrow1, lrow2, w1, w2):
    n, d = x2.shape
    t = MOE_TOKEN_TILE
    tok = lambda i, dst: (i, 0)
    grid_spec = pltpu.PrefetchScalarGridSpec(
        num_scalar_prefetch=1,
        grid=(n // t,),
        in_specs=[
            pl.BlockSpec((t, d), tok),
            pl.BlockSpec((t, 1), tok),
            pl.BlockSpec((t, 1), tok),
            pl.BlockSpec((t, 1), tok),
            pl.BlockSpec((t, 1), tok),
            pl.BlockSpec(memory_space=pl.ANY),
        ],
        out_specs=pl.BlockSpec((t, d), tok),
        scratch_shapes=[pltpu.VMEM((MOE_PIECES * MOE_GROUP_ALIGN, d), BF16), pltpu.SemaphoreType.DMA(())],
    )
    return pl.pallas_call(
        _combine_kernel,
        grid_spec=grid_spec,
        out_shape=jax.ShapeDtypeStruct((n, d), F32),
        compiler_params=_params("arbitrary"),
        name="moe_combine",
    )(piece_dst, x2, lrow1, lrow2, w1, w2, ys)


def _moe(x2, y, wo, gn, router, wg, wu, wd, *, layer):
    n, d = x2.shape
    t = MOE_TOKEN_TILE
    nt = n // t
    n_row_tiles = (2 * n + nt * N_EXPERTS * (MOE_GROUP_ALIGN - 1) + N_EXPERTS * (MOE_ROW_TILE - 1)) // MOE_ROW_TILE
    x2, hn, meta, cnt = _route(x2, y, wo, gn, router, tm=t)
    counts = cnt[:, 0, :N_EXPERTS].astype(I32)
    piece_dst, tile_expert, tile_valid = _moe_tables(counts, n_row_tiles)
    lrow = meta[:, 0:2].astype(I32)
    xs = _dispatch(piece_dst, hn, lrow[:, 0].reshape(nt, 1, t), lrow[:, 1].reshape(nt, 1, t),
                   n_row_tiles * MOE_ROW_TILE)
    ys = _experts(tile_expert, tile_valid, xs, wg, wu, wd, layer=layer, tf=512)
    return _combine(piece_dst, x2, ys, lrow[:, 0:1], lrow[:, 1:2], meta[:, 2:3], meta[:, 3:4])


def _qkv_kernel(x_ref, gn_ref, wqk_ref, wvt_ref, seg_ref, gqk_ref, qk_ref, vt_ref):
    h = _rms(x_ref[...], gn_ref[...]).astype(BF16)
    vt_ref[...] = _dot_nt(wvt_ref[...], h).astype(BF16)
    z = _dot(h, wqk_ref[...])
    seg = seg_ref[...]
    sw = seg.shape[0]
    for c in range(z.shape[1] // sw):
        cols = slice(c * sw, (c + 1) * sw)
        zc = z[:, cols]
        ms = _dot((zc * zc).astype(BF16), seg) * (1.0 / ATT_HEAD_DIM)
        qk_ref[:, cols] = (zc * lax.rsqrt(ms + EPS) * gqk_ref[:, cols]).astype(BF16)


def _qkv(x2, gn, wqk, wvt, seg, gqk, *, bsz, tm):
    n, d = x2.shape
    seq = n // bsz
    qk_cols = wqk.shape[1]
    v_cols = wvt.shape[0]
    per_b = seq // tm
    return pl.pallas_call(
        _qkv_kernel,
        grid=(bsz, per_b),
        in_specs=[
            pl.BlockSpec((tm, d), lambda b, i: (b * per_b + i, 0)),
            pl.BlockSpec((1, d), lambda b, i: (0, 0)),
            pl.BlockSpec((d, qk_cols), lambda b, i: (0, 0)),
            pl.BlockSpec((v_cols, d), lambda b, i: (0, 0)),
            pl.BlockSpec(seg.shape, lambda b, i: (0, 0)),
            pl.BlockSpec((1, qk_cols), lambda b, i: (0, 0)),
        ],
        out_specs=[
            pl.BlockSpec((tm, qk_cols), lambda b, i: (b * per_b + i, 0)),
            pl.BlockSpec((None, v_cols, tm), lambda b, i: (b, 0, i)),
        ],
        out_shape=[
            jax.ShapeDtypeStruct((n, qk_cols), BF16),
            jax.ShapeDtypeStruct((bsz, v_cols, seq), BF16),
        ],
        compiler_params=_params("parallel", "parallel"),
        name="od_qkv",
    )(x2, gn, wqk, wvt, seg, gqk)


def _attn_kernel(qi_ref, ki_ref, scal_ref, q_ref, k_ref, vt_ref, sg_ref, o_ref, qm_ref, l_ref, acc_ref, *maybe_m_ref,
                 bq, bk, out_scale, running_max):
    p = pl.program_id(2)
    qi = qi_ref[p]
    ki = ki_ref[p]
    last_k = ((qi + 1) * bq - 1) // bk

    @pl.when(ki == 0)
    def _():
        q = q_ref[...]
        lane = lax.broadcasted_iota(I32, q.shape, 1)
        zero = jnp.zeros_like(q)
        qm_ref[0] = jnp.where(lane < ATT_HEAD_DIM, q, zero)
        qm_ref[1] = jnp.where(lane >= ATT_HEAD_DIM, q, zero)
        l_ref[...] = jnp.zeros_like(l_ref)
        acc_ref[...] = jnp.zeros_like(acc_ref)
        if running_max:
            maybe_m_ref[0][...] = jnp.full_like(maybe_m_ref[0], -jnp.inf)

    def step(masked):
        k = k_ref[...]
        vt = vt_ref[...]
        if masked:
            kpos = ki * bk + lax.broadcasted_iota(I32, (bk, bq), 0)
            qpos = qi * bq + lax.broadcasted_iota(I32, (bk, bq), 1)
            keep = kpos <= qpos
        for m in range(2):
            s = _dot_nt(k, qm_ref[m])
            if masked:
                s = jnp.where(keep, s, -jnp.inf)
            if running_max:
                m_ref = maybe_m_ref[0]
                m_old = m_ref[m]
                m_new = jnp.maximum(m_old, jnp.max(s, axis=0, keepdims=True))
                pt = jnp.exp2(s - m_new)
                alpha = jnp.exp2(m_old - m_new)
                l_ref[m] = alpha * l_ref[m] + jnp.sum(pt, axis=0, keepdims=True)
                acc_ref[m] = alpha * acc_ref[m] + _dot(vt, pt.astype(BF16))
                m_ref[m] = m_new
            else:
                pt = jnp.exp2(s)
                l_ref[m] += jnp.sum(pt, axis=0, keepdims=True)
                acc_ref[m] += _dot(vt, pt.astype(BF16))

    def step_plain(masked):
        s = _dot_nt(k_ref[...], qm_ref[...].reshape(2 * bq, qm_ref.shape[-1]))
        if masked:
            kpos = ki * bk + lax.broadcasted_iota(I32, s.shape, 0)
            qcol = lax.broadcasted_iota(I32, s.shape, 1)
            qpos = qi * bq + jnp.where(qcol >= bq, qcol - bq, qcol)
            s = jnp.where(kpos <= qpos, s, -jnp.inf)
        pt = jnp.exp2(s)
        lsum = jnp.sum(pt, axis=0, keepdims=True)
        pv = _dot(vt_ref[...], pt.astype(BF16))
        for m in range(2):
            l_ref[m] += lsum[:, m * bq:(m + 1) * bq]
            acc_ref[m] += pv[:, m * bq:(m + 1) * bq]

    if not running_max:
        step = step_plain

    fully_visible = (ki + 1) * bk - 1 <= qi * bq

    @pl.when(fully_visible)
    def _():
        step(False)

    @pl.when(jnp.logical_not(fully_visible))
    def _():
        step(True)

    @pl.when(ki == last_k)
    def _():
        o = acc_ref[0] / l_ref[0] - scal_ref[0] * (acc_ref[1] / l_ref[1])
        ms = jnp.mean(o * o, axis=0, keepdims=True)
        o = o * lax.rsqrt(ms + EPS) * (sg_ref[...] * out_scale)
        o_ref[...] = o.T.astype(BF16)


def _attention(qk3, vt3, scal, subln_col, *, bq, bk, lambda_init, running_max):
    bsz, seq, _ = qk3.shape
    dv = 2 * ATT_HEAD_DIM
    nh = ATT_HEADS
    pairs = [(qi, ki) for qi in range(seq // bq) for ki in range(((qi + 1) * bq - 1) // bk + 1)]
    qi_tab = jnp.asarray(np.array([p[0] for p in pairs], np.int32))
    ki_tab = jnp.asarray(np.array([p[1] for p in pairs], np.int32))
    kern = functools.partial(_attn_kernel, bq=bq, bk=bk, out_scale=1.0 - lambda_init, running_max=running_max)
    scratch = [
        pltpu.VMEM((2, bq, dv), BF16),
        pltpu.VMEM((2, 1, bq), F32),
        pltpu.VMEM((2, dv, bq), F32),
    ]
    if running_max:
        scratch.append(pltpu.VMEM((2, 1, bq), F32))
    grid_spec = pltpu.PrefetchScalarGridSpec(
        num_scalar_prefetch=2,
        grid=(bsz, nh, len(pairs)),
        in_specs=[
            pl.BlockSpec(memory_space=pltpu.SMEM),
            pl.BlockSpec((None, bq, dv), lambda b, h, p, qt, kt: (b, qt[p], h)),
            pl.BlockSpec((None, bk, dv), lambda b, h, p, qt, kt: (b, kt[p], nh + h)),
            pl.BlockSpec((None, dv, bk), lambda b, h, p, qt, kt: (b, h, kt[p])),
            pl.BlockSpec((dv, 1), lambda b, h, p, qt, kt: (0, 0)),
        ],
        out_specs=pl.BlockSpec((None, bq, dv), lambda b, h, p, qt, kt: (b, qt[p], h)),
        scratch_shapes=scratch,
    )
    return pl.pallas_call(
        kern,
        grid_spec=grid_spec,
        out_shape=jax.ShapeDtypeStruct((bsz, seq, nh * dv), BF16),
        compiler_params=_params("parallel", "parallel", "arbitrary"),
        name="od_attn_online" if running_max else "od_attn",
    )(qi_tab, ki_tab, scal, qk3, qk3, vt3, subln_col)


def _row(v):
    return v.astype(F32).reshape(1, -1)


def _pick(n, pref):
    t = min(n, pref)
    assert n % t == 0, (n, pref)
    return t


def kernel(x, ev_norm_mix, ev_w_in, ev_gmlp_ln_g, ev_gmlp_ln_b, ev_gmlp_ws, ev_gmlp_bs, ev_s5_lambda_re, ev_s5_lambda_im, ev_s5_log_dt, ev_s5_b_re, ev_s5_b_im, ev_s5_c_re, ev_s5_c_im, ev_s5_d, ev_s5_w_glu, ev_s5_b_glu, ev_w_out, ev_norm_ffn, ev_ffn_w_gate, ev_ffn_w_up, ev_ffn_w_down, od_norm_mix, od_w_in, od_q_norm, od_k_norm, od_lambda_q1, od_lambda_k1, od_lambda_q2, od_lambda_k2, od_subln, od_w_out, od_norm_ffn, od_router, od_moe_w_gate, od_moe_w_up, od_moe_w_down):
    bsz, seq, d = x.shape
    n = bsz * seq
    depth = ev_norm_mix.shape[0] + od_norm_mix.shape[0]
    a_width = ev_gmlp_ln_g.shape[-1]
    assert seq % (S5_BLOCK * 2) == 0 and seq % GMLP_CHUNK == 0 and n % MOE_TOKEN_TILE == 0
    r = seq // S5_BLOCK
    nsteps = max(1, (r - 1).bit_length())

    tm_in = _pick(seq, 512)
    bq = _pick(seq, 1024)
    bk = _pick(seq, 1024)
    qk_cols = 2 * ATT_HEADS * 2 * ATT_HEAD_DIM
    tril = jnp.tril(jnp.ones((GMLP_CHUNK, GMLP_CHUNK), F32))
    seg_id = jnp.arange(4 * ATT_HEAD_DIM) // ATT_HEAD_DIM
    seg = (seg_id[:, None] == seg_id[None, :]).astype(BF16)

    moe_wg = od_moe_w_gate.astype(BF16)
    moe_wu = od_moe_w_up.astype(BF16)
    moe_wd = od_moe_w_down.astype(BF16)

    x2 = x.reshape(n, d)
    for layer in range(depth):
        i = layer // 2
        if layer % 2 == 0:
            wm = (ev_gmlp_ws[i].astype(F32) * tril).astype(BF16)
            bias = jnp.repeat(ev_gmlp_bs[i].astype(F32).T, a_width // A_HEADS, axis=1)
            ya, zb = _ev_in(x2, _row(ev_norm_mix[i]), ev_w_in[i].astype(BF16), _row(ev_gmlp_ln_g[i]),
                            _row(ev_gmlp_ln_b[i]), wm, bias, tm=tm_in)
            bw = zb.shape[1]
            zbt = zb.reshape(bsz, r, S5_BLOCK, bw).transpose(0, 2, 3, 1)
            ops = _s5_operators(ev_s5_lambda_re[i], ev_s5_lambda_im[i], ev_s5_log_dt[i], ev_s5_b_re[i],
                                ev_s5_b_im[i], ev_s5_c_re[i], ev_s5_c_im[i], ev_s5_d[i], nsteps)
            ybt = _s5(zbt, *ops)
            yb = ybt.transpose(0, 3, 1, 2).reshape(n, bw)
            w_out = ev_w_out[i].astype(BF16)
            x2 = _ev_out_ffn(x2, ya, yb, ev_s5_w_glu[i].astype(BF16), _row(ev_s5_b_glu[i]),
                             w_out[:a_width], w_out[a_width:], _row(ev_norm_ffn[i]),
                             ev_ffn_w_gate[i].astype(BF16), ev_ffn_w_up[i].astype(BF16),
                             ev_ffn_w_down[i].astype(BF16), tm=tm_in, tf=ev_ffn_w_gate.shape[-1] // 2)
        else:
            lambda_init = 0.8 - 0.6 * math.exp(-0.3 * layer)
            gqk = jnp.concatenate([
                jnp.tile(od_q_norm[i].astype(F32) * (ATT_HEAD_DIM ** -0.5 * LOG2E), 2 * ATT_HEADS),
                jnp.tile(od_k_norm[i].astype(F32), 2 * ATT_HEADS)]).reshape(1, -1)
            w_in = od_w_in[i].astype(BF16)
            qk, vt = _qkv(x2, _row(od_norm_mix[i]), w_in[:, :qk_cols], w_in[:, qk_cols:].T, seg, gqk,
                          bsz=bsz, tm=tm_in)
            lam = (jnp.exp(jnp.sum(od_lambda_q1[i].astype(F32) * od_lambda_k1[i].astype(F32)))
                   - jnp.exp(jnp.sum(od_lambda_q2[i].astype(F32) * od_lambda_k2[i].astype(F32)))
                   + lambda_init)
            bound = (ATT_HEAD_DIM * LOG2E * ATT_HEAD_DIM ** -0.5
                     * jnp.max(jnp.abs(od_q_norm[i].astype(F32))) * jnp.max(jnp.abs(od_k_norm[i].astype(F32))))
            scal = jnp.stack([lam, bound])
            attend = functools.partial(_attention, bq=bq, bk=bk, lambda_init=lambda_init)
            yc = lax.cond(bound <= ATT_STATIC_MAX_LIMIT,
                          functools.partial(attend, running_max=False),
                          functools.partial(attend, running_max=True),
                          qk.reshape(bsz, seq, qk_cols), vt, scal, od_subln[i].astype(F32).reshape(-1, 1))
            router = jnp.pad(od_router[i].astype(F32), ((0, 0), (0, ROUTER_LANES - N_EXPERTS)))
            x2 = _moe(x2, yc.reshape(n, -1), od_w_out[i].astype(BF16), _row(od_norm_ffn[i]), router,
                      moe_wg, moe_wu, moe_wd, layer=i)
    return x2.reshape(bsz, seq, d)
```

```python
import functools
import math

import numpy as np
import jax
import jax.numpy as jnp
from jax import lax
from jax.experimental import pallas as pl
from jax.experimental.pallas import tpu as pltpu

F32 = jnp.float32
BF16 = jnp.bfloat16
I32 = jnp.int32

EPS = 1e-6
A_HEADS = 4
GMLP_CHUNK = 128
S5_GROUP = 16
S5_STATE = 64
S5_BLOCK = 16
DT_LAMBDA_RE_MAX = -1e-4
ATT_HEADS = 8
ATT_HEAD_DIM = 64
N_EXPERTS = 8
ROUTER_LANES = 128
MOE_TOKEN_TILE = 512
MOE_ROW_TILE = 1024
MOE_GROUP_ALIGN = 16
MOE_PIECES = -(-(2 * MOE_TOKEN_TILE + N_EXPERTS * (MOE_GROUP_ALIGN - 1)) // MOE_GROUP_ALIGN)
MOE_RUN_ROWS = tuple(MOE_GROUP_ALIGN << k for k in reversed(range((MOE_TOKEN_TILE // MOE_GROUP_ALIGN).bit_length())))
MOE_META_ROWS = 8
LOG2E = 1.4426950408889634
ATT_STATIC_MAX_LIMIT = 40.0
VMEM_LIMIT = 56 * 1024 * 1024


def _params(*sem):
    return pltpu.CompilerParams(dimension_semantics=sem, vmem_limit_bytes=VMEM_LIMIT)


def _rms(x, g):
    return x * lax.rsqrt(jnp.mean(x * x, axis=-1, keepdims=True) + EPS) * g


def _gelu(x):
    return 0.5 * x * (1.0 + jnp.tanh(0.7978845608028654 * (x + 0.044715 * (x * x * x))))


def _sigmoid(x):
    return 1.0 / (1.0 + jnp.exp(-x))


def _dot(a, b):
    return jnp.dot(a, b, preferred_element_type=F32)


def _dot_nt(a, b):
    return lax.dot_general(a, b, (((1,), (1,)), ((), ())), preferred_element_type=F32)


def _ev_in_kernel(x_ref, gn_ref, w_ref, lng_ref, lnb_ref, wm_ref, bias_ref, ya_ref, zb_ref, *, n_chunk, a_width):
    h = _rms(x_ref[...], gn_ref[...]).astype(BF16)
    z = _dot(h, w_ref[...])
    zb_ref[...] = z[:, 2 * a_width:].astype(BF16)
    g = _gelu(z[:, :2 * a_width])
    u = g[:, :a_width]
    v = g[:, a_width:]
    mu = jnp.mean(v, axis=-1, keepdims=True)
    vc = v - mu
    vn = vc * lax.rsqrt(jnp.mean(vc * vc, axis=-1, keepdims=True) + EPS) * lng_ref[...] + lnb_ref[...]
    vb = vn.astype(BF16)
    hd = a_width // A_HEADS
    for c in range(n_chunk):
        rows = slice(c * GMLP_CHUNK, (c + 1) * GMLP_CHUNK)
        for k in range(A_HEADS):
            cols = slice(k * hd, (k + 1) * hd)
            s = _dot(wm_ref[k], vb[rows, cols]) + bias_ref[:, cols]
            ya_ref[rows, cols] = (u[rows, cols] * s).astype(BF16)


def _ev_in(x2, gn, w_in, ln_g, ln_b, wm, bias, *, tm):
    n, d = x2.shape
    cols = w_in.shape[1]
    a_width = ln_g.shape[-1]
    b_width = cols - 2 * a_width
    kern = functools.partial(_ev_in_kernel, n_chunk=tm // GMLP_CHUNK, a_width=a_width)
    return pl.pallas_call(
        kern,
        grid=(n // tm,),
        in_specs=[
            pl.BlockSpec((tm, d), lambda i: (i, 0)),
            pl.BlockSpec((1, d), lambda i: (0, 0)),
            pl.BlockSpec((d, cols), lambda i: (0, 0)),
            pl.BlockSpec((1, a_width), lambda i: (0, 0)),
            pl.BlockSpec((1, a_width), lambda i: (0, 0)),
            pl.BlockSpec((A_HEADS, GMLP_CHUNK, GMLP_CHUNK), lambda i: (0, 0, 0)),
            pl.BlockSpec((GMLP_CHUNK, a_width), lambda i: (0, 0)),
        ],
        out_specs=[
            pl.BlockSpec((tm, a_width), lambda i: (i, 0)),
            pl.BlockSpec((tm, b_width), lambda i: (i, 0)),
        ],
        out_shape=[
            jax.ShapeDtypeStruct((n, a_width), BF16),
            jax.ShapeDtypeStruct((n, b_width), BF16),
        ],
        compiler_params=_params("parallel"),
        name="ev_in_gmlp",
    )(x2, gn, w_in, ln_g, ln_b, wm, bias)


def _s5_kernel(u_ref, tt_ref, pt_ref, qt_ref, ar_ref, ai_ref, o_ref, *, nb, nsteps):
    r = u_ref.shape[-1]
    rows = S5_BLOCK * S5_GROUP
    lane = lax.broadcasted_iota(jnp.int32, (S5_STATE, r), 1)

    def shifted(a, sh):
        return jnp.where(lane >= sh, pltpu.roll(a, sh, 1), 0.0)

    for b in range(nb):
        ut = u_ref[b].reshape(rows, r)
        st = _dot(pt_ref[...], ut)
        xr = st[:S5_STATE]
        xi = st[S5_STATE:]
        for j in range(nsteps):
            sh = 1 << j
            ar = ar_ref[j]
            ai = ai_ref[j]
            sr = shifted(xr, sh)
            si = shifted(xi, sh)
            xr, xi = xr + ar * sr - ai * si, xi + ar * si + ai * sr
        xprev = jnp.concatenate([shifted(xr, 1), shifted(xi, 1)], axis=0).astype(BF16)
        y = _dot(tt_ref[...], ut) + _dot(qt_ref[...], xprev)
        o_ref[b] = _gelu(y).astype(BF16).reshape(S5_BLOCK, S5_GROUP, r)


def _s5(zbt, tt, pt, qt, ar, ai):
    bsz, _, width, r = zbt.shape
    groups = width // S5_GROUP
    nsteps = ar.shape[1]
    u5 = zbt.reshape(bsz, S5_BLOCK, groups, S5_GROUP, r)
    rows = S5_BLOCK * S5_GROUP
    kern = functools.partial(_s5_kernel, nb=bsz, nsteps=nsteps)
    out = pl.pallas_call(
        kern,
        grid=(groups,),
        in_specs=[
            pl.BlockSpec((bsz, S5_BLOCK, None, S5_GROUP, r), lambda g: (0, 0, g, 0, 0)),
            pl.BlockSpec((None, rows, rows), lambda g: (g, 0, 0)),
            pl.BlockSpec((None, 2 * S5_STATE, rows), lambda g: (g, 0, 0)),
            pl.BlockSpec((None, rows, 2 * S5_STATE), lambda g: (g, 0, 0)),
            pl.BlockSpec((None, nsteps, S5_STATE, 1), lambda g: (g, 0, 0, 0)),
            pl.BlockSpec((None, nsteps, S5_STATE, 1), lambda g: (g, 0, 0, 0)),
        ],
        out_specs=pl.BlockSpec((bsz, S5_BLOCK, None, S5_GROUP, r), lambda g: (0, 0, g, 0, 0)),
        out_shape=jax.ShapeDtypeStruct(u5.shape, BF16),
        compiler_params=_params("parallel"),
        name="ev_s5",
    )(u5, tt, pt, qt, ar, ai)
    return out.reshape(zbt.shape)


def _s5_operators(lam_re, lam_im, log_dt, b_re, b_im, c_re, c_im, d, nsteps):
    groups, p = lam_re.shape
    c = S5_GROUP
    ln = S5_BLOCK
    lr = jnp.minimum(lam_re.astype(F32), DT_LAMBDA_RE_MAX)
    li = lam_im.astype(F32)
    dt = jnp.exp(log_dt.astype(F32))[:, None]
    zr = lr * dt
    zi = li * dt

    def powers(k):
        kk = jnp.asarray(k, F32)[None, :, None]
        mag = jnp.exp(zr[:, None, :] * kk)
        ang = zi[:, None, :] * kk
        return mag * jnp.cos(ang), mag * jnp.sin(ang)

    lbr, lbi = powers(jnp.ones((1,)))
    lbr = lbr[:, 0]
    lbi = lbi[:, 0]
    den = lr * lr + li * li
    fr = ((lbr - 1.0) * lr + lbi * li) / den
    fi = (lbi * lr - (lbr - 1.0) * li) / den
    bbr = fr[..., None] * b_re.astype(F32) - fi[..., None] * b_im.astype(F32)
    bbi = fr[..., None] * b_im.astype(F32) + fi[..., None] * b_re.astype(F32)
    cr = c_re.astype(F32)
    ci = c_im.astype(F32)

    pwr, pwi = powers(jnp.arange(ln + 1))
    mr = pwr[:, :ln, :, None] * bbr[:, None] - pwi[:, :ln, :, None] * bbi[:, None]
    mi = pwr[:, :ln, :, None] * bbi[:, None] + pwi[:, :ln, :, None] * bbr[:, None]
    kmat = jnp.einsum("gcp,gkpd->gkcd", cr, mr) - jnp.einsum("gcp,gkpd->gkcd", ci, mi)
    s_idx = jnp.arange(ln)[:, None]
    t_idx = jnp.arange(ln)[None, :]
    lag = t_idx - s_idx
    blocks = kmat[:, jnp.clip(lag, 0, ln - 1)]
    blocks = jnp.where((lag >= 0)[None, :, :, None, None], blocks, 0.0)
    skip = (jnp.eye(ln, dtype=F32)[None, :, :, None, None]
            * jnp.eye(c, dtype=F32)[None, None, None]
            * d.astype(F32).reshape(groups, 1, 1, c, 1))
    blocks = blocks + skip
    tt = blocks.transpose(0, 2, 3, 1, 4).reshape(groups, ln * c, ln * c)

    rev = ln - 1 - jnp.arange(ln)
    pr = pwr[:, rev][:, :, :, None] * bbr[:, None] - pwi[:, rev][:, :, :, None] * bbi[:, None]
    pi = pwr[:, rev][:, :, :, None] * bbi[:, None] + pwi[:, rev][:, :, :, None] * bbr[:, None]
    pt = jnp.concatenate([pr, pi], axis=2).transpose(0, 2, 1, 3).reshape(groups, 2 * p, ln * c)

    a_r = cr[:, None] * pwr[:, 1:ln + 1, None, :] - ci[:, None] * pwi[:, 1:ln + 1, None, :]
    a_i = cr[:, None] * pwi[:, 1:ln + 1, None, :] + ci[:, None] * pwr[:, 1:ln + 1, None, :]
    qt = jnp.concatenate([a_r, -a_i], axis=-1).reshape(groups, ln * c, 2 * p)

    sr, si = powers(ln * (2 ** jnp.arange(nsteps)))
    return (tt.astype(BF16), pt.astype(BF16), qt.astype(BF16), sr[..., None], si[..., None])


def _ev_out_ffn_kernel(x_ref, ya_ref, yb_ref, wglu_ref, bglu_ref, woa_ref, wob_ref, gn_ref, wg_ref, wu_ref, wd_ref,
                       o_ref, xn_ref, hn_ref, acc_ref):
    f = pl.program_id(1)

    @pl.when(f == 0)
    def _():
        yb = yb_ref[...]
        gate = _sigmoid(_dot(yb, wglu_ref[...]) + bglu_ref[...])
        y2 = (yb.astype(F32) * gate).astype(BF16)
        xn = x_ref[...] + _dot(ya_ref[...], woa_ref[...]) + _dot(y2, wob_ref[...])
        xn_ref[...] = xn
        hn_ref[...] = _rms(xn, gn_ref[...]).astype(BF16)
        acc_ref[...] = jnp.zeros_like(acc_ref)

    h = hn_ref[...]
    a = _dot(h, wg_ref[...])
    b = _dot(h, wu_ref[...])
    acc_ref[...] += _dot((a * _sigmoid(a) * b).astype(BF16), wd_ref[...])

    @pl.when(f == pl.num_programs(1) - 1)
    def _():
        o_ref[...] = xn_ref[...] + acc_ref[...]


def _ev_out_ffn(x2, ya, yb, wglu, bglu, woa, wob, gn, wg, wu, wd, *, tm, tf):
    n, d = x2.shape
    aw = ya.shape[1]
    bw = yb.shape[1]
    dff = wg.shape[1]
    tok = lambda i, f: (i, 0)
    const = lambda i, f: (0, 0)
    return pl.pallas_call(
        _ev_out_ffn_kernel,
        grid=(n // tm, dff // tf),
        in_specs=[
            pl.BlockSpec((tm, d), tok),
            pl.BlockSpec((tm, aw), tok),
            pl.BlockSpec((tm, bw), tok),
            pl.BlockSpec((bw, bw), const),
            pl.BlockSpec((1, bw), const),
            pl.BlockSpec((aw, d), const),
            pl.BlockSpec((bw, d), const),
            pl.BlockSpec((1, d), const),
            pl.BlockSpec((d, tf), lambda i, f: (0, f)),
            pl.BlockSpec((d, tf), lambda i, f: (0, f)),
            pl.BlockSpec((tf, d), lambda i, f: (f, 0)),
        ],
        out_specs=pl.BlockSpec((tm, d), tok),
        out_shape=jax.ShapeDtypeStruct((n, d), F32),
        scratch_shapes=[pltpu.VMEM((tm, d), F32), pltpu.VMEM((tm, d), BF16), pltpu.VMEM((tm, d), F32)],
        compiler_params=_params("parallel", "arbitrary"),
        name="ev_out_ffn",
    )(x2, ya, yb, wglu, bglu, woa, wob, gn, wg, wu, wd)


def _route_kernel(x_ref, y_ref, wo_ref, gn_ref, router_ref, triu_ref, xn_ref, hn_ref, meta_ref, cnt_ref):
    xn = x_ref[...] + _dot(y_ref[...], wo_ref[...])
    xn_ref[...] = xn
    hn = _rms(xn, gn_ref[...])
    hn_ref[...] = hn.astype(BF16)
    logits = lax.dot_general(router_ref[...], hn, (((1,), (1,)), ((), ())), preferred_element_type=F32,
                             precision=lax.Precision.HIGHEST)
    e_idx = lax.broadcasted_iota(I32, logits.shape, 0)
    neg = jnp.float32(-jnp.inf)
    m1 = jnp.max(logits, axis=0, keepdims=True)
    i1 = jnp.min(jnp.where(logits == m1, e_idx, N_EXPERTS), axis=0, keepdims=True)
    rest = jnp.where(e_idx == i1, neg, logits)
    m2 = jnp.max(rest, axis=0, keepdims=True)
    i2 = jnp.min(jnp.where(rest == m2, e_idx, N_EXPERTS), axis=0, keepdims=True)
    e2 = jnp.exp(m2 - m1)
    w1 = 1.0 / (1.0 + e2)
    w2 = e2 / (1.0 + e2)
    chosen = jnp.where(e_idx == i1, 1.0, jnp.where(e_idx == i2, 1.0, 0.0))
    before = _dot(chosen.astype(BF16), triu_ref[...])
    cnt = jnp.sum(chosen, axis=1, keepdims=True)
    cnt_ref[...] = jnp.broadcast_to(cnt, cnt_ref.shape)
    cpad = jnp.ceil(cnt * (1.0 / MOE_GROUP_ALIGN)) * MOE_GROUP_ALIGN
    e_row = lax.broadcasted_iota(I32, (N_EXPERTS, N_EXPERTS), 0)
    e_col = lax.broadcasted_iota(I32, (N_EXPERTS, N_EXPERTS), 1)
    lower = jnp.where(e_col < e_row, 1.0, 0.0).astype(BF16)
    lstart = _dot(lower, jnp.broadcast_to(cpad, logits.shape).astype(BF16))
    row_of = before + lstart
    r1 = jnp.sum(jnp.where(e_idx == i1, row_of, 0.0), axis=0, keepdims=True)
    r2 = jnp.sum(jnp.where(e_idx == i2, row_of, 0.0), axis=0, keepdims=True)
    pad = jnp.zeros((meta_ref.shape[0] - 4, logits.shape[1]), F32)
    meta_ref[...] = jnp.concatenate([r1, r2, w1, w2, pad], axis=0)


def _route(x2, y, wo, gn, router_t, *, tm):
    n, d = x2.shape
    nt = n // tm
    triu = jnp.triu(jnp.ones((tm, tm), F32), 1).astype(BF16)
    return pl.pallas_call(
        _route_kernel,
        grid=(nt,),
        in_specs=[
            pl.BlockSpec((tm, d), lambda i: (i, 0)),
            pl.BlockSpec((tm, y.shape[1]), lambda i: (i, 0)),
            pl.BlockSpec(wo.shape, lambda i: (0, 0)),
            pl.BlockSpec((1, d), lambda i: (0, 0)),
            pl.BlockSpec((N_EXPERTS, d), lambda i: (0, 0)),
            pl.BlockSpec((tm, tm), lambda i: (0, 0)),
        ],
        out_specs=[
            pl.BlockSpec((tm, d), lambda i: (i, 0)),
            pl.BlockSpec((tm, d), lambda i: (i, 0)),
            pl.BlockSpec((None, MOE_META_ROWS, tm), lambda i: (i, 0, 0)),
            pl.BlockSpec((None, N_EXPERTS, ROUTER_LANES), lambda i: (i, 0, 0)),
        ],
        out_shape=[
            jax.ShapeDtypeStruct((n, d), F32),
            jax.ShapeDtypeStruct((n, d), BF16),
            jax.ShapeDtypeStruct((nt, MOE_META_ROWS, tm), F32),
            jax.ShapeDtypeStruct((nt, N_EXPERTS, ROUTER_LANES), F32),
        ],
        compiler_params=_params("parallel"),
        name="moe_route",
    )(x2, y, wo, gn, router_t, triu)


def _moe_tables(counts, n_row_tiles):
    g = MOE_GROUP_ALIGN
    tf = MOE_ROW_TILE
    cpad = ((counts + g - 1) // g) * g
    tot = cpad.sum(0)
    padded = ((tot + tf - 1) // tf) * tf
    off = jnp.cumsum(padded) - padded
    gstart = off[None] + jnp.cumsum(cpad, axis=0) - cpad
    lend = jnp.cumsum(cpad, axis=1)
    lstart = lend - cpad
    groups = jnp.stack([lstart, gstart, cpad], axis=-1).astype(I32).ravel()
    tile_start = jnp.arange(n_row_tiles, dtype=I32) * tf
    ends = (off + padded).astype(I32)
    tile_valid = (tile_start < ends[-1]).astype(I32)
    tile_expert = jnp.sum(tile_start[:, None] >= ends[None, :], axis=1).astype(I32)
    last_expert = jnp.max(jnp.where(tot > 0, jnp.arange(N_EXPERTS), 0)).astype(I32)
    tile_expert = jnp.where(tile_valid > 0, tile_expert, last_expert)
    return groups, tile_expert, tile_valid


def _for_each_run(grp_ref, tile, buf_ref, rows_ref, sem, *, to_rows, act):
    g = MOE_GROUP_ALIGN
    for e in range(N_EXPERTS):
        base = (tile * N_EXPERTS + e) * 3
        lo = grp_ref[base]
        go = grp_ref[base + 1]
        size = grp_ref[base + 2]
        for rows in MOE_RUN_ROWS:
            @pl.when((size & rows) != 0)
            def _():
                done = size & (-2 * rows)
                local = buf_ref.at[pl.ds(pl.multiple_of(lo + done, g), rows)]
                remote = rows_ref.at[pl.ds(pl.multiple_of(go + done, g), rows)]
                act(pltpu.make_async_copy(local, remote, sem) if to_rows
                    else pltpu.make_async_copy(remote, local, sem))


def _dispatch_kernel(grp_ref, hn_ref, meta_ref, xs_in_ref, xs_ref, buf_ref, sems):
    del xs_in_ref
    i = pl.program_id(0)
    slot = lax.rem(i, 2)
    row = lax.broadcasted_iota(I32, (buf_ref.shape[1], hn_ref.shape[0]), 0)
    l1 = meta_ref[0:1, :].astype(I32)
    l2 = meta_ref[1:2, :].astype(I32)
    sel = jnp.where(row == l1, 1.0, jnp.where(row == l2, 1.0, 0.0))
    buf_ref[slot] = _dot(sel.astype(BF16), hn_ref[...]).astype(BF16)

    def runs(tile, s, act):
        _for_each_run(grp_ref, tile, buf_ref.at[s], xs_ref, sems.at[s], to_rows=True, act=act)

    runs(i, slot, lambda cp: cp.start())

    @pl.when(i > 0)
    def _():
        runs(i - 1, 1 - slot, lambda cp: cp.wait())

    @pl.when(i == pl.num_programs(0) - 1)
    def _():
        runs(i, slot, lambda cp: cp.wait())


def _dispatch(groups, hn, meta, n_rows):
    n, d = hn.shape
    t = MOE_TOKEN_TILE
    xs0 = jnp.zeros((n_rows, d), BF16)
    grid_spec = pltpu.PrefetchScalarGridSpec(
        num_scalar_prefetch=1,
        grid=(n // t,),
        in_specs=[
            pl.BlockSpec((t, d), lambda i, dst: (i, 0)),
            pl.BlockSpec((None, MOE_META_ROWS, t), lambda i, dst: (i, 0, 0)),
            pl.BlockSpec(memory_space=pl.ANY),
        ],
        out_specs=pl.BlockSpec(memory_space=pl.ANY),
        scratch_shapes=[pltpu.VMEM((2, MOE_PIECES * MOE_GROUP_ALIGN, d), BF16), pltpu.SemaphoreType.DMA((2,))],
    )
    return pl.pallas_call(
        _dispatch_kernel,
        grid_spec=grid_spec,
        out_shape=jax.ShapeDtypeStruct((n_rows, d), BF16),
        input_output_aliases={3: 0},
        compiler_params=_params("arbitrary"),
        name="moe_dispatch",
    )(groups, hn, meta, xs0)


def _experts_kernel(te_ref, tv_ref, xs_ref, wg_ref, wu_ref, wd_ref, ys_ref, acc_ref):
    del te_ref
    j = pl.program_id(0)
    f = pl.program_id(1)
    valid = tv_ref[j] > 0

    @pl.when(f == 0)
    def _():
        acc_ref[...] = jnp.zeros_like(acc_ref)

    @pl.when(valid)
    def _():
        h = xs_ref[...]
        a = _dot(h, wg_ref[...].astype(BF16))
        b = _dot(h, wu_ref[...].astype(BF16))
        acc_ref[...] += _dot((a * _sigmoid(a) * b).astype(BF16), wd_ref[...].astype(BF16))

    @pl.when(f == pl.num_programs(1) - 1)
    def _():
        ys_ref[...] = acc_ref[...].astype(BF16)


def _experts(tile_expert, tile_valid, xs, wg, wu, wd, *, layer, tf):
    n_rows, d = xs.shape
    dff = wg.shape[3]
    tm = MOE_ROW_TILE
    grid_spec = pltpu.PrefetchScalarGridSpec(
        num_scalar_prefetch=2,
        grid=(n_rows // tm, dff // tf),
        in_specs=[
            pl.BlockSpec((tm, d), lambda j, f, te, tv: (j, 0)),
            pl.BlockSpec((None, None, d, tf), lambda j, f, te, tv: (layer, te[j], 0, f * tv[j])),
            pl.BlockSpec((None, None, d, tf), lambda j, f, te, tv: (layer, te[j], 0, f * tv[j])),
            pl.BlockSpec((None, None, tf, d), lambda j, f, te, tv: (layer, te[j], f * tv[j], 0)),
        ],
        out_specs=pl.BlockSpec((tm, d), lambda j, f, te, tv: (j, 0)),
        scratch_shapes=[pltpu.VMEM((tm, d), F32)],
    )
    return pl.pallas_call(
        _experts_kernel,
        grid_spec=grid_spec,
        out_shape=jax.ShapeDtypeStruct((n_rows, d), BF16),
        compiler_params=_params("parallel", "arbitrary"),
        name="moe_experts",
    )(tile_expert, tile_valid, xs, wg, wu, wd)


def _combine_kernel(grp_ref, x_ref, meta_ref, ys_ref, o_ref, buf_ref, sems):
    i = pl.program_id(0)
    g = MOE_GROUP_ALIGN
    slot = lax.rem(i, 2)

    def fetch(tile, s):
        last = (tile * N_EXPERTS + N_EXPERTS - 1) * 3
        used = grp_ref[last] + grp_ref[last + 2]
        for q in range(2 * x_ref.shape[0] // g, MOE_PIECES):
            @pl.when(q * g >= used)
            def _():
                buf_ref[s, q * g:(q + 1) * g, :] = jnp.zeros((g, buf_ref.shape[2]), BF16)
        _for_each_run(grp_ref, tile, buf_ref.at[s], ys_ref, sems.at[s], to_rows=False, act=lambda cp: cp.start())

    @pl.when(i == 0)
    def _():
        fetch(0, 0)

    @pl.when(i + 1 < pl.num_programs(0))
    def _():
        fetch(i + 1, 1 - slot)

    row = lax.broadcasted_iota(I32, (buf_ref.shape[1], x_ref.shape[0]), 0)
    l1 = meta_ref[0:1, :].astype(I32)
    l2 = meta_ref[1:2, :].astype(I32)
    sel = jnp.where(row == l1, meta_ref[2:3, :], jnp.where(row == l2, meta_ref[3:4, :], 0.0)).astype(BF16)
    _for_each_run(grp_ref, i, buf_ref.at[slot], ys_ref, sems.at[slot], to_rows=False, act=lambda cp: cp.wait())
    upd = lax.dot_general(sel, buf_ref[slot], (((0,), (0,)), ((), ())), preferred_element_type=F32)
    o_ref[...] = x_ref[...] + upd


def _combine(groups, x2, ys, meta):
    n, d = x2.shape
    t = MOE_TOKEN_TILE
    grid_spec = pltpu.PrefetchScalarGridSpec(
        num_scalar_prefetch=1,
        grid=(n // t,),
        in_specs=[
            pl.BlockSpec((t, d), lambda i, dst: (i, 0)),
            pl.BlockSpec((None, MOE_META_ROWS, t), lambda i, dst: (i, 0, 0)),
            pl.BlockSpec(memory_space=pl.ANY),
        ],
        out_specs=pl.BlockSpec((t, d), lambda i, dst: (i, 0)),
        scratch_shapes=[pltpu.VMEM((2, MOE_PIECES * MOE_GROUP_ALIGN, d), BF16), pltpu.SemaphoreType.DMA((2,))],
    )
    return pl.pallas_call(
        _combine_kernel,
        grid_spec=grid_spec,
        out_shape=jax.ShapeDtypeStruct((n, d), F32),
        compiler_params=_params("arbitrary"),
        name="moe_combine",
    )(groups, x2, meta, ys)


def _moe(x2, y, wo, gn, router_t, wg, wu, wd, *, layer):
    n, d = x2.shape
    t = MOE_TOKEN_TILE
    nt = n // t
    n_row_tiles = (2 * n + nt * N_EXPERTS * (MOE_GROUP_ALIGN - 1) + N_EXPERTS * (MOE_ROW_TILE - 1)) // MOE_ROW_TILE
    x2, hn, meta, cnt = _route(x2, y, wo, gn, router_t, tm=t)
    groups, tile_expert, tile_valid = _moe_tables(cnt[:, :, 0].astype(I32), n_row_tiles)
    xs = _dispatch(groups, hn, meta, n_row_tiles * MOE_ROW_TILE)
    ys = _experts(tile_expert, tile_valid, xs, wg, wu, wd, layer=layer, tf=512)
    return _combine(groups, x2, ys, meta)


def _qkv_kernel(x_ref, gn_ref, wqk_ref, wvt_ref, seg_ref, gqk_ref, qk_ref, vt_ref):
    h = _rms(x_ref[...], gn_ref[...]).astype(BF16)
    vt_ref[...] = _dot_nt(wvt_ref[...], h).astype(BF16)
    z = _dot(h, wqk_ref[...])
    seg = seg_ref[...]
    sw = seg.shape[0]
    for c in range(z.shape[1] // sw):
        cols = slice(c * sw, (c + 1) * sw)
        zc = z[:, cols]
        ms = _dot((zc * zc).astype(BF16), seg) * (1.0 / ATT_HEAD_DIM)
        qk_ref[:, cols] = (zc * lax.rsqrt(ms + EPS) * gqk_ref[:, cols]).astype(BF16)


def _qkv(x2, gn, wqk, wvt, seg, gqk, *, bsz, tm):
    n, d = x2.shape
    seq = n // bsz
    qk_cols = wqk.shape[1]
    v_cols = wvt.shape[0]
    per_b = seq // tm
    return pl.pallas_call(
        _qkv_kernel,
        grid=(bsz, per_b),
        in_specs=[
            pl.BlockSpec((tm, d), lambda b, i: (b * per_b + i, 0)),
            pl.BlockSpec((1, d), lambda b, i: (0, 0)),
            pl.BlockSpec((d, qk_cols), lambda b, i: (0, 0)),
            pl.BlockSpec((v_cols, d), lambda b, i: (0, 0)),
            pl.BlockSpec(seg.shape, lambda b, i: (0, 0)),
            pl.BlockSpec((1, qk_cols), lambda b, i: (0, 0)),
        ],
        out_specs=[
            pl.BlockSpec((tm, qk_cols), lambda b, i: (b * per_b + i, 0)),
            pl.BlockSpec((None, v_cols, tm), lambda b, i: (b, 0, i)),
        ],
        out_shape=[
            jax.ShapeDtypeStruct((n, qk_cols), BF16),
            jax.ShapeDtypeStruct((bsz, v_cols, seq), BF16),
        ],
        compiler_params=_params("parallel", "parallel"),
        name="od_qkv",
    )(x2, gn, wqk, wvt, seg, gqk)


def _attn_kernel(qi_ref, ki_ref, scal_ref, q_ref, k_ref, vt_ref, sg_ref, o_ref, qm_ref, l_ref, acc_ref, *maybe_m_ref,
                 bq, bk, out_scale, running_max):
    p = pl.program_id(2)
    qi = qi_ref[p]
    ki = ki_ref[p]
    last_k = ((qi + 1) * bq - 1) // bk

    @pl.when(ki == 0)
    def _():
        q = q_ref[...]
        lane = lax.broadcasted_iota(I32, q.shape, 1)
        zero = jnp.zeros_like(q)
        qm_ref[0] = jnp.where(lane < ATT_HEAD_DIM, q, zero)
        qm_ref[1] = jnp.where(lane >= ATT_HEAD_DIM, q, zero)
        l_ref[...] = jnp.zeros_like(l_ref)
        acc_ref[...] = jnp.zeros_like(acc_ref)
        if running_max:
            maybe_m_ref[0][...] = jnp.full_like(maybe_m_ref[0], -jnp.inf)

    def step(masked):
        k = k_ref[...]
        vt = vt_ref[...]
        if masked:
            kpos = ki * bk + lax.broadcasted_iota(I32, (bk, bq), 0)
            qpos = qi * bq + lax.broadcasted_iota(I32, (bk, bq), 1)
            keep = kpos <= qpos
        for m in range(2):
            s = _dot_nt(k, qm_ref[m])
            if masked:
                s = jnp.where(keep, s, -jnp.inf)
            if running_max:
                m_ref = maybe_m_ref[0]
                m_old = m_ref[m]
                m_new = jnp.maximum(m_old, jnp.max(s, axis=0, keepdims=True))
                pt = jnp.exp2(s - m_new)
                alpha = jnp.exp2(m_old - m_new)
                l_ref[m] = alpha * l_ref[m] + jnp.sum(pt, axis=0, keepdims=True)
                acc_ref[m] = alpha * acc_ref[m] + _dot(vt, pt.astype(BF16))
                m_ref[m] = m_new
            else:
                pt = jnp.exp2(s)
                l_ref[m] += jnp.sum(pt, axis=0, keepdims=True)
                acc_ref[m] += _dot(vt, pt.astype(BF16))

    def step_plain(masked):
        s = _dot_nt(k_ref[...], qm_ref[...].reshape(2 * bq, qm_ref.shape[-1]))
        if masked:
            kpos = ki * bk + lax.broadcasted_iota(I32, s.shape, 0)
            qcol = lax.broadcasted_iota(I32, s.shape, 1)
            qpos = qi * bq + jnp.where(qcol >= bq, qcol - bq, qcol)
            s = jnp.where(kpos <= qpos, s, -jnp.inf)
        pt = jnp.exp2(s)
        lsum = jnp.sum(pt, axis=0, keepdims=True)
        pv = _dot(vt_ref[...], pt.astype(BF16))
        for m in range(2):
            l_ref[m] += lsum[:, m * bq:(m + 1) * bq]
            acc_ref[m] += pv[:, m * bq:(m + 1) * bq]

    if not running_max:
        step = step_plain

    fully_visible = (ki + 1) * bk - 1 <= qi * bq

    @pl.when(fully_visible)
    def _():
        step(False)

    @pl.when(jnp.logical_not(fully_visible))
    def _():
        step(True)

    @pl.when(ki == last_k)
    def _():
        o = acc_ref[0] / l_ref[0] - scal_ref[0] * (acc_ref[1] / l_ref[1])
        ms = jnp.mean(o * o, axis=0, keepdims=True)
        o = o * lax.rsqrt(ms + EPS) * (sg_ref[...] * out_scale)
        o_ref[...] = o.T.astype(BF16)


def _attention(qk3, vt3, scal, subln_col, *, bq, bk, lambda_init, running_max):
    bsz, seq, _ = qk3.shape
    dv = 2 * ATT_HEAD_DIM
    nh = ATT_HEADS
    pairs = [(qi, ki) for qi in range(seq // bq) for ki in range(((qi + 1) * bq - 1) // bk + 1)]
    qi_tab = jnp.asarray(np.array([p[0] for p in pairs], np.int32))
    ki_tab = jnp.asarray(np.array([p[1] for p in pairs], np.int32))
    kern = functools.partial(_attn_kernel, bq=bq, bk=bk, out_scale=1.0 - lambda_init, running_max=running_max)
    scratch = [
        pltpu.VMEM((2, bq, dv), BF16),
        pltpu.VMEM((2, 1, bq), F32),
        pltpu.VMEM((2, dv, bq), F32),
    ]
    if running_max:
        scratch.append(pltpu.VMEM((2, 1, bq), F32))
    grid_spec = pltpu.PrefetchScalarGridSpec(
        num_scalar_prefetch=2,
        grid=(bsz, nh, len(pairs)),
        in_specs=[
            pl.BlockSpec(memory_space=pltpu.SMEM),
            pl.BlockSpec((None, bq, dv), lambda b, h, p, qt, kt: (b, qt[p], h)),
            pl.BlockSpec((None, bk, dv), lambda b, h, p, qt, kt: (b, kt[p], nh + h)),
            pl.BlockSpec((None, dv, bk), lambda b, h, p, qt, kt: (b, h, kt[p])),
            pl.BlockSpec((dv, 1), lambda b, h, p, qt, kt: (0, 0)),
        ],
        out_specs=pl.BlockSpec((None, bq, dv), lambda b, h, p, qt, kt: (b, qt[p], h)),
        scratch_shapes=scratch,
    )
    return pl.pallas_call(
        kern,
        grid_spec=grid_spec,
        out_shape=jax.ShapeDtypeStruct((bsz, seq, nh * dv), BF16),
        compiler_params=_params("parallel", "parallel", "arbitrary"),
        name="od_attn_online" if running_max else "od_attn",
    )(qi_tab, ki_tab, scal, qk3, qk3, vt3, subln_col)


def _row(v):
    return v.astype(F32).reshape(1, -1)


def _pick(n, pref):
    t = min(n, pref)
    assert n % t == 0, (n, pref)
    return t


def kernel(x, ev_norm_mix, ev_w_in, ev_gmlp_ln_g, ev_gmlp_ln_b, ev_gmlp_ws, ev_gmlp_bs, ev_s5_lambda_re, ev_s5_lambda_im, ev_s5_log_dt, ev_s5_b_re, ev_s5_b_im, ev_s5_c_re, ev_s5_c_im, ev_s5_d, ev_s5_w_glu, ev_s5_b_glu, ev_w_out, ev_norm_ffn, ev_ffn_w_gate, ev_ffn_w_up, ev_ffn_w_down, od_norm_mix, od_w_in, od_q_norm, od_k_norm, od_lambda_q1, od_lambda_k1, od_lambda_q2, od_lambda_k2, od_subln, od_w_out, od_norm_ffn, od_router, od_moe_w_gate, od_moe_w_up, od_moe_w_down):
    bsz, seq, d = x.shape
    n = bsz * seq
    depth = ev_norm_mix.shape[0] + od_norm_mix.shape[0]
    a_width = ev_gmlp_ln_g.shape[-1]
    assert seq % (S5_BLOCK * 2) == 0 and seq % GMLP_CHUNK == 0 and n % MOE_TOKEN_TILE == 0
    r = seq // S5_BLOCK
    nsteps = max(1, (r - 1).bit_length())

    tm_in = _pick(seq, 512)
    bq = _pick(seq, 1024)
    bk = _pick(seq, 1024)
    qk_cols = 2 * ATT_HEADS * 2 * ATT_HEAD_DIM
    tril = jnp.tril(jnp.ones((GMLP_CHUNK, GMLP_CHUNK), F32))
    seg_id = jnp.arange(4 * ATT_HEAD_DIM) // ATT_HEAD_DIM
    seg = (seg_id[:, None] == seg_id[None, :]).astype(BF16)

    x2 = x.reshape(n, d)
    for layer in range(depth):
        i = layer // 2
        if layer % 2 == 0:
            wm = (ev_gmlp_ws[i].astype(F32) * tril).astype(BF16)
            bias = jnp.repeat(ev_gmlp_bs[i].astype(F32).T, a_width // A_HEADS, axis=1)
            ya, zb = _ev_in(x2, _row(ev_norm_mix[i]), ev_w_in[i].astype(BF16), _row(ev_gmlp_ln_g[i]),
                            _row(ev_gmlp_ln_b[i]), wm, bias, tm=tm_in)
            bw = zb.shape[1]
            zbt = zb.reshape(bsz, r, S5_BLOCK, bw).transpose(0, 2, 3, 1)
            ops = _s5_operators(ev_s5_lambda_re[i], ev_s5_lambda_im[i], ev_s5_log_dt[i], ev_s5_b_re[i],
                                ev_s5_b_im[i], ev_s5_c_re[i], ev_s5_c_im[i], ev_s5_d[i], nsteps)
            ybt = _s5(zbt, *ops)
            yb = ybt.transpose(0, 3, 1, 2).reshape(n, bw)
            w_out = ev_w_out[i].astype(BF16)
            x2 = _ev_out_ffn(x2, ya, yb, ev_s5_w_glu[i].astype(BF16), _row(ev_s5_b_glu[i]),
                             w_out[:a_width], w_out[a_width:], _row(ev_norm_ffn[i]),
                             ev_ffn_w_gate[i].astype(BF16), ev_ffn_w_up[i].astype(BF16),
                             ev_ffn_w_down[i].astype(BF16), tm=tm_in, tf=ev_ffn_w_gate.shape[-1] // 2)
        else:
            lambda_init = 0.8 - 0.6 * math.exp(-0.3 * layer)
            gqk = jnp.concatenate([
                jnp.tile(od_q_norm[i].astype(F32) * (ATT_HEAD_DIM ** -0.5 * LOG2E), 2 * ATT_HEADS),
                jnp.tile(od_k_norm[i].astype(F32), 2 * ATT_HEADS)]).reshape(1, -1)
            w_in = od_w_in[i].astype(BF16)
            qk, vt = _qkv(x2, _row(od_norm_mix[i]), w_in[:, :qk_cols], w_in[:, qk_cols:].T, seg, gqk,
                          bsz=bsz, tm=tm_in)
            lam = (jnp.exp(jnp.sum(od_lambda_q1[i].astype(F32) * od_lambda_k1[i].astype(F32)))
                   - jnp.exp(jnp.sum(od_lambda_q2[i].astype(F32) * od_lambda_k2[i].astype(F32)))
                   + lambda_init)
            bound = (ATT_HEAD_DIM * LOG2E * ATT_HEAD_DIM ** -0.5
                     * jnp.max(jnp.abs(od_q_norm[i].astype(F32))) * jnp.max(jnp.abs(od_k_norm[i].astype(F32))))
            scal = jnp.stack([lam, bound])
            attend = functools.partial(_attention, bq=bq, bk=bk, lambda_init=lambda_init)
            yc = lax.cond(bound <= ATT_STATIC_MAX_LIMIT,
                          functools.partial(attend, running_max=False),
                          functools.partial(attend, running_max=True),
                          qk.reshape(bsz, seq, qk_cols), vt, scal, od_subln[i].astype(F32).reshape(-1, 1))
            x2 = _moe(x2, yc.reshape(n, -1), od_w_out[i].astype(BF16), _row(od_norm_ffn[i]),
                      od_router[i].astype(F32).T, od_moe_w_gate, od_moe_w_up, od_moe_w_down, layer=i)
    return x2.reshape(bsz, seq, d)
```
